```python
import jax, jax.numpy as jnp
from jax import lax
import numpy as np

D_MODEL = 1024
BATCH = 32
SEQ = 2048
DEPTH = 4

CTX_LEN = 256
GRID_W = 64
N_BRANCH = 3
NA_HEADS = 8
NA_DH = 64
NA_KH = 8
NA_KW = 16
MLA_HEADS = 8
MLA_NOPE = 64
MLA_ROPE = 32
MLA_V = 64
MLA_Q_RANK = 384
MLA_KV_RANK = 256
MLA_QBLOCK = 128
LRU_W = 512
LRU_BLOCKS = 8
LRU_BS = LRU_W // LRU_BLOCKS
LRU_C = 8.0
CONV_W = 4
D_FF = 4 * D_MODEL
ROPE_BASE = 10000.0
EPS = 1e-6
NEG = -1e30

NA_WIDTH = NA_HEADS * NA_DH
MLA_WIDTH = MLA_HEADS * MLA_V
MLA_QK = MLA_NOPE + MLA_ROPE
SPLITS = (NA_WIDTH, NA_WIDTH, NA_WIDTH, MLA_Q_RANK, MLA_KV_RANK, MLA_ROPE, LRU_W, LRU_W, N_BRANCH * D_MODEL)
IN_COLS = sum(SPLITS)

kernel_name = "hybrid_na_mla_rglru_prefix_block"


def rmsnorm(x, g):
    xf = x.astype(jnp.float32)
    y = xf * lax.rsqrt(jnp.mean(xf * xf, axis=-1, keepdims=True) + EPS)
    return (y * g.astype(jnp.float32)).astype(x.dtype)


def split_cols(p):
    idx = np.cumsum(np.array(SPLITS))[:-1].tolist()
    return jnp.split(p, idx, axis=-1)


def attend(q, k, v, scale):
    s = jnp.einsum('bqhd,bkhd->bhqk', q, k).astype(jnp.float32) * scale
    p = jax.nn.softmax(s, axis=-1).astype(v.dtype)
    return jnp.einsum('bhqk,bkhd->bqhd', p, v)


def axial_rope(z):
    T = z.shape[1]
    t = jnp.arange(T)
    row = (t // GRID_W).astype(jnp.float32)
    col = (t % GRID_W).astype(jnp.float32)
    half = z.shape[-1] // 2
    nf = half // 2
    inv = ROPE_BASE ** (-jnp.arange(nf, dtype=jnp.float32) / nf)

    def rot(u, pos):
        ang = pos[:, None] * inv
        cos = jnp.cos(ang)[None, :, None, :]
        sin = jnp.sin(ang)[None, :, None, :]
        uf = u.astype(jnp.float32)
        u1, u2 = uf[..., :nf], uf[..., nf:]
        return jnp.concatenate([u1 * cos - u2 * sin, u1 * sin + u2 * cos], axis=-1).astype(u.dtype)

    return jnp.concatenate([rot(z[..., :half], row), rot(z[..., half:], col)], axis=-1)


def rope_tail(z):
    return jnp.concatenate([z[..., :MLA_NOPE], axial_rope(z[..., MLA_NOPE:])], axis=-1)


def na_latent(q, k, v, kc, vc, rpb):
    B, S, H, dh = q.shape
    rows = S // GRID_W
    kh = min(NA_KH, rows)
    ncb = GRID_W // NA_KW
    scale = dh ** -0.5
    qg = q.reshape(B, rows, GRID_W, H, dh)
    kg = k.reshape(B, rows, GRID_W, H, dh)
    vg = v.reshape(B, rows, GRID_W, H, dh)
    col_q = jnp.arange(GRID_W).reshape(ncb, NA_KW)
    blk_start = jnp.clip(jnp.arange(ncb) * NA_KW - NA_KW // 2, 0, GRID_W - 2 * NA_KW)
    key_cols = blk_start[:, None] + jnp.arange(2 * NA_KW)
    cs = jnp.clip(col_q - NA_KW // 2, 0, GRID_W - NA_KW)
    kcol = key_cols[:, None, :]
    col_valid = (kcol >= cs[..., None]) & (kcol < cs[..., None] + NA_KW)
    dc_idx = jnp.clip(kcol - col_q[..., None] + NA_KW - 1, 0, 2 * NA_KW - 2)
    n_win = kh * 2 * NA_KW

    def row_fn(r):
        rs = jnp.clip(r - kh // 2, 0, rows - kh)
        qr = lax.dynamic_index_in_dim(qg, r, axis=1, keepdims=False).reshape(B, ncb, NA_KW, H, dh)
        kr = lax.dynamic_slice_in_dim(kg, rs, kh, axis=1)[:, :, key_cols]
        vr = lax.dynamic_slice_in_dim(vg, rs, kh, axis=1)[:, :, key_cols]
        s_win = jnp.einsum('bnqhd,bmnjhd->bhnqmj', qr, kr).astype(jnp.float32) * scale
        dr_idx = rs + jnp.arange(kh) - r + NA_KH - 1
        bias = rpb[:, dr_idx[None, None, :, None], dc_idx[:, :, None, :]]
        s_win = jnp.where(col_valid[:, :, None, :], s_win + bias.astype(jnp.float32), NEG)
        s_ctx = jnp.einsum('bnqhd,blhd->bhnql', qr, kc).astype(jnp.float32) * scale
        s = jnp.concatenate([s_win.reshape(B, H, ncb, NA_KW, n_win), s_ctx], axis=-1)
        p = jax.nn.softmax(s, axis=-1).astype(v.dtype)
        p_win = p[..., :n_win].reshape(B, H, ncb, NA_KW, kh, 2 * NA_KW)
        p_ctx = p[..., n_win:]
        o = jnp.einsum('bhnqmj,bmnjhd->bnqhd', p_win, vr) + jnp.einsum('bhnql,blhd->bnqhd', p_ctx, vc)
        return o.reshape(B, GRID_W, H * dh)

    out = lax.map(row_fn, jnp.arange(rows))
    return jnp.transpose(out, (1, 0, 2, 3)).reshape(B, S, H * dh)


def mla_q(pq, qa_g, w_qb, q_g):
    B, T, _ = pq.shape
    q = (rmsnorm(pq, qa_g) @ w_qb).reshape(B, T, MLA_HEADS, MLA_QK)
    return rmsnorm(q, q_g)


def mla_kv(pkv, pr, kva_g, w_kvb, k_g):
    B, T, _ = pkv.shape
    kv = (rmsnorm(pkv, kva_g) @ w_kvb).reshape(B, T, MLA_HEADS, MLA_NOPE + MLA_V)
    k_nope, v = kv[..., :MLA_NOPE], kv[..., MLA_NOPE:]
    k_r = jnp.broadcast_to(pr[:, :, None, :], (B, T, MLA_HEADS, MLA_ROPE))
    k = rmsnorm(jnp.concatenate([k_nope, k_r], axis=-1), k_g)
    return k, v


def mla_latent(q, k, v, kc, vc):
    B, S, H, dq = q.shape
    kall = jnp.concatenate([k, kc], axis=1)
    vall = jnp.concatenate([v, vc], axis=1)
    nb = S // MLA_QBLOCK
    qb = q.reshape(B, nb, MLA_QBLOCK, H, dq).swapaxes(0, 1)
    o = lax.map(lambda qi: attend(qi, kall, vall, dq ** -0.5), qb)
    return o.swapaxes(0, 1).reshape(B, S, H * MLA_V)


def conv_centred(u, w, b):
    T = u.shape[1]
    left = CONV_W // 2
    up = jnp.pad(u, ((0, 0), (left, CONV_W - 1 - left), (0, 0)))
    return b + sum(up[:, j:j + T] * w[j] for j in range(CONV_W))


def rglru_ab(u, wa, ba, wx, bx, lam):
    B, T, W = u.shape
    uf = u.astype(jnp.float32)
    ub = uf.reshape(B, T, LRU_BLOCKS, LRU_BS)
    r = jax.nn.sigmoid(jnp.einsum('btnc,ncd->btnd', ub, wa.astype(jnp.float32)).reshape(B, T, W) + ba.astype(jnp.float32))
    ig = jax.nn.sigmoid(jnp.einsum('btnc,ncd->btnd', ub, wx.astype(jnp.float32)).reshape(B, T, W) + bx.astype(jnp.float32))
    log_a = -LRU_C * r * jax.nn.softplus(-lam.astype(jnp.float32))
    a = jnp.exp(log_a)
    bterm = jnp.sqrt(-jnp.expm1(2.0 * log_a)) * ig * uf
    return a, bterm


def lin_scan(a, b, h0, reverse):
    if h0 is not None:
        idx = -1 if reverse else 0
        b = b.at[:, idx].add(a[:, idx] * h0)

    def comb(l, r):
        al, bl = l
        ar, br = r
        return al * ar, ar * bl + br

    _, h = lax.associative_scan(comb, (a, b), reverse=reverse, axis=1)
    return h


def token_mixers(hx, hc, w_in, na_qg, na_kg, rpb, qa_g, w_qb, kva_g, w_kvb, q_g, k_g,
                 conv_w, conv_b, wa, ba, wx, bx, lam, w_na_o, w_mla_o, w_lru_o, w_o, need_ctx):
    B, S, _ = hx.shape
    L = hc.shape[1]
    nqx, nkx, nvx, mqx, mkvx, mrx, lux, lgx, gtx = split_cols(hx @ w_in)
    nqc, nkc, nvc, mqc, mkvc, mrc, luc, lgc, gtc = split_cols(hc @ w_in)
    heads = lambda p: p.reshape(p.shape[0], p.shape[1], NA_HEADS, NA_DH)

    qx = rmsnorm(heads(nqx), na_qg)
    kx = rmsnorm(heads(nkx), na_kg)
    vx = heads(nvx)
    kc = rmsnorm(heads(nkc), na_kg)
    vc = heads(nvc)
    o_na_x = na_latent(qx, kx, vx, kc, vc, rpb)

    q_mx = rope_tail(mla_q(mqx, qa_g, w_qb, q_g))
    k_mx, v_mx = mla_kv(mkvx, mrx, kva_g, w_kvb, k_g)
    k_mx = rope_tail(k_mx)
    k_mc, v_mc = mla_kv(mkvc, mrc, kva_g, w_kvb, k_g)
    o_mla_x = mla_latent(q_mx, k_mx, v_mx, k_mc, v_mc)

    ux = conv_centred(lux, conv_w, conv_b)
    uc = conv_centred(luc, conv_w, conv_b)
    hsum_x = None
    hsum_c = None
    for d, rev in ((0, False), (1, True)):
        a_c, b_c = rglru_ab(uc, wa[d], ba[d], wx[d], bx[d], lam[d])
        h_c = lin_scan(a_c, b_c, None, rev)
        a_x, b_x = rglru_ab(ux, wa[d], ba[d], wx[d], bx[d], lam[d])
        h_x = lin_scan(a_x, b_x, h_c[:, 0 if rev else -1], rev)
        hsum_x = h_x if hsum_x is None else hsum_x + h_x
        if need_ctx:
            hsum_c = h_c if hsum_c is None else hsum_c + h_c
    o_lru_x = jax.nn.gelu(lgx) * hsum_x.astype(lgx.dtype)

    g_na, g_mla, g_lru = jnp.split(jax.nn.sigmoid(gtx), N_BRANCH, axis=-1)
    yx = (g_na * (o_na_x @ w_na_o) + g_mla * (o_mla_x @ w_mla_o) + g_lru * (o_lru_x @ w_lru_o)) @ w_o
    if not need_ctx:
        return yx, None

    qc = rmsnorm(heads(nqc), na_qg)
    o_na_c = attend(qc, kc, vc, NA_DH ** -0.5).reshape(B, L, NA_WIDTH)
    q_mc = mla_q(mqc, qa_g, w_qb, q_g)
    o_mla_c = attend(q_mc, k_mc, v_mc, MLA_QK ** -0.5).reshape(B, L, MLA_WIDTH)
    o_lru_c = jax.nn.gelu(lgc) * hsum_c.astype(lgc.dtype)
    gc_na, gc_mla, gc_lru = jnp.split(jax.nn.sigmoid(gtc), N_BRANCH, axis=-1)
    yc = (gc_na * (o_na_c @ w_na_o) + gc_mla * (o_mla_c @ w_mla_o) + gc_lru * (o_lru_c @ w_lru_o)) @ w_o
    return yx, yc


def sq_relu_mlp(h, w1, w2):
    return jnp.square(jax.nn.relu(h @ w1)) @ w2


def setup_inputs(seed: int = 0) -> dict:
    key = jax.random.key(seed)
    ks = list(jax.random.split(key, 40))
    cnt = [0]

    def nrm(shape, s):
        k = ks[cnt[0]]
        cnt[0] += 1
        return jax.random.normal(k, shape, jnp.float32) * s

    Dp = DEPTH
    a_c = jax.random.uniform(ks[39], (Dp, 2, LRU_W), jnp.float32, 0.9, 0.999)
    a0 = a_c ** (1.0 / LRU_C)
    lam = jnp.log(a0) - jnp.log1p(-a0)
    return {
        "x": nrm((BATCH, SEQ, D_MODEL), 1.0),
        "c": nrm((BATCH, D_MODEL), 1.0),
        "ctx": nrm((BATCH, CTX_LEN, D_MODEL), 1.0),
        "c_ctx": nrm((D_MODEL,), 1.0),
        "w_mod": nrm((Dp, D_MODEL, 6 * D_MODEL), 0.5 * D_MODEL ** -0.5),
        "b_mod": nrm((Dp, 6 * D_MODEL), 0.02),
        "g_mix": 1.0 + nrm((Dp, D_MODEL), 0.02),
        "g_mlp": 1.0 + nrm((Dp, D_MODEL), 0.02),
        "w_in": nrm((Dp, D_MODEL, IN_COLS), D_MODEL ** -0.5),
        "na_q_gain": 1.0 + nrm((Dp, NA_DH), 0.02),
        "na_k_gain": 1.0 + nrm((Dp, NA_DH), 0.02),
        "na_rpb": nrm((Dp, NA_HEADS, 2 * NA_KH - 1, 2 * NA_KW - 1), 0.1),
        "mla_qa_gain": 1.0 + nrm((Dp, MLA_Q_RANK), 0.02),
        "w_q_b": nrm((Dp, MLA_Q_RANK, MLA_HEADS * MLA_QK), MLA_Q_RANK ** -0.5),
        "mla_kva_gain": 1.0 + nrm((Dp, MLA_KV_RANK), 0.02),
        "w_kv_b": nrm((Dp, MLA_KV_RANK, MLA_HEADS * (MLA_NOPE + MLA_V)), MLA_KV_RANK ** -0.5),
        "mla_q_gain": 1.0 + nrm((Dp, MLA_QK), 0.02),
        "mla_k_gain": 1.0 + nrm((Dp, MLA_QK), 0.02),
        "lru_conv_w": nrm((Dp, CONV_W, LRU_W), CONV_W ** -0.5),
        "lru_conv_b": nrm((Dp, LRU_W), 0.02),
        "lru_wa": nrm((Dp, 2, LRU_BLOCKS, LRU_BS, LRU_BS), LRU_BS ** -0.5),
        "lru_ba": nrm((Dp, 2, LRU_W), 0.02),
        "lru_wx": nrm((Dp, 2, LRU_BLOCKS, LRU_BS, LRU_BS), LRU_BS ** -0.5),
        "lru_bx": nrm((Dp, 2, LRU_W), 0.02),
        "lru_lambda": lam,
        "w_na_o": nrm((Dp, NA_WIDTH, D_MODEL), NA_WIDTH ** -0.5),
        "w_mla_o": nrm((Dp, MLA_WIDTH, D_MODEL), MLA_WIDTH ** -0.5),
        "w_lru_o": nrm((Dp, LRU_W, D_MODEL), LRU_W ** -0.5),
        "w_o": nrm((Dp, D_MODEL, D_MODEL), D_MODEL ** -0.5),
        "w_ff1": nrm((Dp, D_MODEL, D_FF), D_MODEL ** -0.5),
        "w_ff2": nrm((Dp, D_FF, D_MODEL), D_FF ** -0.5),
    }


def reference(x, c, ctx, c_ctx, w_mod, b_mod, g_mix, g_mlp, w_in, na_q_gain, na_k_gain, na_rpb,
              mla_qa_gain, w_q_b, mla_kva_gain, w_kv_b, mla_q_gain, mla_k_gain,
              lru_conv_w, lru_conv_b, lru_wa, lru_ba, lru_wx, lru_bx, lru_lambda,
              w_na_o, w_mla_o, w_lru_o, w_o, w_ff1, w_ff2):
    s_lat = jax.nn.silu(c)
    s_ctx = jax.nn.silu(c_ctx)
    for i in range(DEPTH):
        need_ctx = i < DEPTH - 1
        mx = (s_lat @ w_mod[i] + b_mod[i])[:, None, :]
        mc = s_ctx @ w_mod[i] + b_mod[i]
        sh1x, sc1x, g1x, sh2x, sc2x, g2x = jnp.split(mx, 6, axis=-1)
        sh1c, sc1c, g1c, sh2c, sc2c, g2c = jnp.split(mc, 6, axis=-1)
        hx = rmsnorm(x, g_mix[i]) * (1.0 + sc1x) + sh1x
        hc = rmsnorm(ctx, g_mix[i]) * (1.0 + sc1c) + sh1c
        yx, yc = token_mixers(hx, hc, w_in[i], na_q_gain[i], na_k_gain[i], na_rpb[i],
                              mla_qa_gain[i], w_q_b[i], mla_kva_gain[i], w_kv_b[i], mla_q_gain[i], mla_k_gain[i],
                              lru_conv_w[i], lru_conv_b[i], lru_wa[i], lru_ba[i], lru_wx[i], lru_bx[i], lru_lambda[i],
                              w_na_o[i], w_mla_o[i], w_lru_o[i], w_o[i], need_ctx)
        x = x + g1x * yx
        hx2 = rmsnorm(x, g_mlp[i]) * (1.0 + sc2x) + sh2x
        x = x + g2x * sq_relu_mlp(hx2, w_ff1[i], w_ff2[i])
        if need_ctx:
            ctx = ctx + g1c * yc
            hc2 = rmsnorm(ctx, g_mlp[i]) * (1.0 + sc2c) + sh2c
            ctx = ctx + g2c * sq_relu_mlp(hc2, w_ff1[i], w_ff2[i])
    return x
```

```python
import functools

import numpy as np
import jax
import jax.numpy as jnp
from jax import lax
from jax.experimental import pallas as pl
from jax.experimental.pallas import tpu as pltpu

F32 = jnp.float32
BF16 = jnp.bfloat16

D_MODEL = 1024
CTX_LEN = 256
GRID_W = 64
NA_HEADS = 8
NA_DH = 64
NA_KH = 8
NA_KW = 16
NA_WIDTH = NA_HEADS * NA_DH
MLA_HEADS = 8
MLA_NOPE = 64
MLA_ROPE = 32
MLA_V = 64
MLA_QK = MLA_NOPE + MLA_ROPE
MLA_Q_RANK = 384
MLA_KV_RANK = 256
MLA_WIDTH = MLA_HEADS * MLA_V
LRU_W = 512
LRU_BLOCKS = 8
LRU_BS = LRU_W // LRU_BLOCKS
LRU_C = 8.0
CONV_W = 4
D_FF = 4 * D_MODEL
ROPE_BASE = 10000.0
EPS = 1e-6
NEG = -1e30

LANES = 128
SUBLANES = 8
TM = 256
MLA_GROUP = LANES
VMEM_LIMIT = 56 * 1024 * 1024

C_NAQ, C_NAK, C_NAV = 0, 512, 1024
C_MQ = 1536
C_MKV = C_MQ + MLA_Q_RANK
C_MR = C_MKV + MLA_KV_RANK
C_LU = C_MR + LANES
C_LG = C_LU + LRU_W
C_GT = C_LG + LRU_W
IN_COLS_PAD = C_GT + 3 * D_MODEL


def _sigmoid(z):
    return 1.0 / (1.0 + jnp.exp(-z))


def _dot(a, b):
    return jnp.dot(a, b, preferred_element_type=F32)


def _dot_t(a, b):
    return lax.dot_general(a, b, (((1,), (1,)), ((), ())), preferred_element_type=F32)


def _const_spec(shape):
    nd = len(shape)
    return pl.BlockSpec(shape, lambda *_: (0,) * nd, pipeline_mode=pl.Buffered(1))


def _mod_kernel(c_ref, w_ref, b_ref, o_ref):
    cv = c_ref[...]
    s = cv * _sigmoid(cv)
    o_ref[0] = _dot(s.astype(BF16), w_ref[0].astype(BF16)) + b_ref[0]


def _modulation(cvec, w_mod, b_mod):
    depth = w_mod.shape[0]
    rp = cvec.shape[0]
    nj = 6 * D_MODEL // D_MODEL
    return pl.pallas_call(
        _mod_kernel,
        grid=(depth, nj),
        in_specs=[
            pl.BlockSpec((rp, D_MODEL), lambda i, j: (0, 0)),
            pl.BlockSpec((1, D_MODEL, D_MODEL), lambda i, j: (i, 0, j)),
            pl.BlockSpec((1, 1, D_MODEL), lambda i, j: (i, 0, j)),
        ],
        out_specs=pl.BlockSpec((1, rp, D_MODEL), lambda i, j: (i, 0, j)),
        out_shape=jax.ShapeDtypeStruct((depth, rp, 6 * D_MODEL), F32),
        name="modulation",
    )(cvec, w_mod, b_mod.reshape(depth, 1, 6 * D_MODEL))


def _head_norm64(p, gain, scale):
    lo = lax.broadcasted_iota(jnp.int32, (p.shape[0], LANES), 1) < NA_DH
    outs = []
    for j in range(NA_WIDTH // LANES):
        blk = p[:, LANES * j:LANES * (j + 1)]
        sq = blk * blk
        s_lo = jnp.sum(jnp.where(lo, sq, 0.0), axis=-1, keepdims=True)
        s_hi = jnp.sum(jnp.where(lo, 0.0, sq), axis=-1, keepdims=True)
        inv = jnp.where(lo, lax.rsqrt(s_lo * (1.0 / NA_DH) + EPS), lax.rsqrt(s_hi * (1.0 / NA_DH) + EPS))
        outs.append(blk * inv * (gain * scale))
    return jnp.concatenate(outs, axis=-1)


def _mla_head_norm_rope(blk, gain, rc, ra, rb, scale):
    ss = jnp.sum(blk * blk, axis=-1, keepdims=True) * (1.0 / MLA_QK)
    n = blk * lax.rsqrt(ss + EPS) * gain
    half = MLA_ROPE // 4
    r = n * rc + pltpu.roll(n, LANES - half, 1) * ra + pltpu.roll(n, half, 1) * rb
    return r * scale


def _inproj_kernel(x_ref, mod_ref, gmix_ref, w_ref, naqg_ref, nakg_ref, qag_ref, wqb_ref, kvag_ref, wk_ref, wv_ref,
                   mqg_ref, mkg_ref, rc_ref, ra_ref, rb_ref,
                   naq_ref, nak_ref, nav_ref, qm_ref, km_ref, vm_ref, lu_ref, lg_ref, gt_ref):
    x = x_ref[0]
    ms = jnp.mean(x * x, axis=-1, keepdims=True)
    y = x * lax.rsqrt(ms + EPS) * gmix_ref[...]
    h = (y * (1.0 + mod_ref[0, 1:2, :]) + mod_ref[0, 0:1, :]).astype(BF16)

    def proj(a, b):
        return _dot(h, w_ref[:, a:b])

    naq_ref[0] = _head_norm64(proj(C_NAQ, C_NAK), naqg_ref[...], NA_DH ** -0.5).astype(BF16)
    nak_ref[0] = _head_norm64(proj(C_NAK, C_NAV), nakg_ref[...], 1.0).astype(BF16)
    nav_ref[0] = proj(C_NAV, C_MQ).astype(BF16)

    rc, ra, rb = rc_ref[...], ra_ref[...], rb_ref[...]

    pq = proj(C_MQ, C_MKV)
    nq = pq * lax.rsqrt(jnp.mean(pq * pq, axis=-1, keepdims=True) + EPS) * qag_ref[...]
    q0 = _dot(nq.astype(BF16), wqb_ref[...])
    for hd in range(MLA_HEADS):
        sl = slice(MLA_GROUP * hd, MLA_GROUP * (hd + 1))
        qm_ref[0, :, sl] = _mla_head_norm_rope(q0[:, sl], mqg_ref[...], rc, ra, rb, MLA_QK ** -0.5).astype(BF16)

    pkv = proj(C_MKV, C_MR)
    nkv = (pkv * lax.rsqrt(jnp.mean(pkv * pkv, axis=-1, keepdims=True) + EPS) * kvag_ref[...]).astype(BF16)
    vm_ref[0] = _dot(nkv, wv_ref[...]).astype(BF16)
    k0 = _dot(nkv, wk_ref[...])
    kr = proj(C_MR, C_LU)
    for hd in range(MLA_HEADS):
        sl = slice(MLA_GROUP * hd, MLA_GROUP * (hd + 1))
        km_ref[0, :, sl] = _mla_head_norm_rope(k0[:, sl] + kr, mkg_ref[...], rc, ra, rb, 1.0).astype(BF16)

    lu_ref[0] = proj(C_LU, C_LG)
    lg_ref[0] = proj(C_LG, C_GT)
    for g in range(3):
        gt_ref[0, :, D_MODEL * g:D_MODEL * (g + 1)] = proj(C_GT + D_MODEL * g, C_GT + D_MODEL * (g + 1)).astype(BF16)


def _mod_index(nb):
    return lambda b, t: (jnp.where(t == 0, nb, b), 0, 0)


def _inproj(xs, mod, gmix, w, naqg, nakg, qag, wqb, kvag, wk, wv, mqg, mkg, rc, ra, rb):
    nb, tt, _ = xs.shape
    nt = tt // TM
    tok = lambda width: pl.BlockSpec((1, TM, width), lambda b, t: (b, t, 0))
    rope = pl.BlockSpec((TM, LANES), lambda b, t: (t, 0))
    out_widths = (NA_WIDTH, NA_WIDTH, NA_WIDTH, MLA_HEADS * MLA_GROUP, MLA_HEADS * MLA_GROUP, MLA_WIDTH,
                  LRU_W, LRU_W, 3 * D_MODEL)
    out_dtypes = (BF16, BF16, BF16, BF16, BF16, BF16, F32, F32, BF16)
    return pl.pallas_call(
        _inproj_kernel,
        grid=(nb, nt),
        in_specs=[
            tok(D_MODEL),
            pl.BlockSpec((1, 6, D_MODEL), _mod_index(nb)),
            _const_spec(gmix.shape), _const_spec(w.shape), _const_spec(naqg.shape), _const_spec(nakg.shape),
            _const_spec(qag.shape), _const_spec(wqb.shape), _const_spec(kvag.shape), _const_spec(wk.shape),
            _const_spec(wv.shape), _const_spec(mqg.shape), _const_spec(mkg.shape),
            rope, rope, rope,
        ],
        out_specs=[tok(wd) for wd in out_widths],
        out_shape=[jax.ShapeDtypeStruct((nb, tt, wd), dt) for wd, dt in zip(out_widths, out_dtypes)],
        compiler_params=pltpu.CompilerParams(vmem_limit_bytes=VMEM_LIMIT),
        name="inproj",
    )(xs, mod, gmix, w, naqg, nakg, qag, wqb, kvag, wk, wv, mqg, mkg, rc, ra, rb)


NA_CTX_BLOCKS = CTX_LEN // GRID_W
NA_WIN = NA_KH * GRID_W


def _softmax_pv(scores, values):
    m = functools.reduce(jnp.maximum, [jnp.max(s, axis=-1, keepdims=True) for s in scores])
    ps = [jnp.exp(s - m) for s in scores]
    l = functools.reduce(jnp.add, [jnp.sum(p, axis=-1, keepdims=True) for p in ps])
    o = functools.reduce(jnp.add, [_dot(p.astype(BF16), v) for p, v in zip(ps, values)])
    return o * (1.0 / l)


def _na_kernel(q_ref, k_ref, v_ref, bias_ref, o_ref, *, t_off, with_ctx):
    t = pl.program_id(1) + t_off
    lo = lax.broadcasted_iota(jnp.int32, (GRID_W, LANES), 1) < NA_DH

    def stacked_q(j):
        q = q_ref[0, :, LANES * j:LANES * (j + 1)]
        return jnp.concatenate([jnp.where(lo, q, 0), jnp.where(lo, 0, q)], axis=0)

    def store(j, o):
        o_ref[0, :, LANES * j:LANES * (j + 1)] = jnp.where(lo, o[:GRID_W], o[GRID_W:]).astype(BF16)

    def ctx_block():
        for j in range(NA_WIDTH // LANES):
            cols = slice(LANES * j, LANES * (j + 1))
            s_c = _dot_t(stacked_q(j), k_ref[0, 0:CTX_LEN, cols])
            store(j, _softmax_pv([s_c], [v_ref[0, 0:CTX_LEN, cols]]))

    def latent_row():
        r = t - NA_CTX_BLOCKS
        rs = jnp.clip(r - NA_KH // 2, 0, GRID_W // 2 - NA_KH)
        w0 = pl.multiple_of(CTX_LEN + rs * GRID_W, GRID_W)
        for j in range(NA_WIDTH // LANES):
            cols = slice(LANES * j, LANES * (j + 1))
            qs = stacked_q(j)
            s_c = _dot_t(qs, k_ref[0, 0:CTX_LEN, cols])
            bias = bias_ref[0, 2 * j:2 * j + 2].reshape(2 * GRID_W, NA_WIN)
            s_w = _dot_t(qs, k_ref[0, pl.ds(w0, NA_WIN), cols]) + bias
            store(j, _softmax_pv([s_c, s_w], [v_ref[0, 0:CTX_LEN, cols], v_ref[0, pl.ds(w0, NA_WIN), cols]]))

    if with_ctx:
        pl.when(t < NA_CTX_BLOCKS)(ctx_block)
        pl.when(t >= NA_CTX_BLOCKS)(latent_row)
    else:
        latent_row()


def _na_bias_index(t_off):
    def index(b, i):
        r = i + t_off - NA_CTX_BLOCKS
        rs = jnp.clip(r - NA_KH // 2, 0, GRID_W // 2 - NA_KH)
        return (jnp.where(r < 0, NA_KH // 2 - 1, rs - r + NA_KH - 1), 0, 0, 0)
    return index


def _na_attention(q, k, v, bias, with_ctx):
    nb, tt, _ = q.shape
    t_off = 0 if with_ctx else NA_CTX_BLOCKS
    nsteps = tt // GRID_W - t_off
    full = pl.BlockSpec((1, tt, NA_WIDTH), lambda b, i: (b, 0, 0))
    blk = pl.BlockSpec((1, GRID_W, NA_WIDTH), lambda b, i: (b, i + t_off, 0))
    return pl.pallas_call(
        functools.partial(_na_kernel, t_off=t_off, with_ctx=with_ctx),
        grid=(nb, nsteps),
        in_specs=[blk, full, full, pl.BlockSpec((1, NA_HEADS, GRID_W, NA_WIN), _na_bias_index(t_off))],
        out_specs=blk,
        out_shape=jax.ShapeDtypeStruct((nb, tt, NA_WIDTH), BF16),
        compiler_params=pltpu.CompilerParams(vmem_limit_bytes=VMEM_LIMIT),
        name="na_attention",
    )(q, k, v, bias)


def _na_bias_table(rpb):
    qc = np.arange(GRID_W)[:, None]
    kc = np.arange(GRID_W)[None, :]
    cs = np.clip(qc - NA_KW // 2, 0, GRID_W - NA_KW)
    valid = (kc >= cs) & (kc < cs + NA_KW)
    dc = np.clip(kc - qc + NA_KW - 1, 0, 2 * NA_KW - 2)
    dr = np.arange(NA_KH)[:, None] + np.arange(NA_KH)[None, :]
    tab = rpb[:, dr][:, :, :, dc]
    tab = jnp.where(valid[None, None, None], tab, NEG)
    tab = jnp.transpose(tab, (1, 0, 3, 2, 4))
    return tab.reshape(NA_KH, NA_HEADS, GRID_W, NA_WIN).astype(F32)


def _mla_kernel(q_ref, k_ref, v_ref, o_ref, *, t_off, with_ctx):
    t = pl.program_id(1) + t_off
    lo = lax.broadcasted_iota(jnp.int32, (TM, LANES), 1) < MLA_V

    def run(nk):
        for j in range(MLA_WIDTH // LANES):
            outs = []
            for hd in (2 * j, 2 * j + 1):
                hs = slice(MLA_GROUP * hd, MLA_GROUP * (hd + 1))
                s = _dot_t(q_ref[0, :, hs], k_ref[0, 0:nk, hs])
                outs.append(_softmax_pv([s], [v_ref[0, 0:nk, LANES * j:LANES * (j + 1)]]))
            o_ref[0, :, LANES * j:LANES * (j + 1)] = jnp.where(lo, outs[0], outs[1]).astype(BF16)

    if with_ctx:
        pl.when(t == 0)(lambda: run(CTX_LEN))
        pl.when(t > 0)(lambda: run(k_ref.shape[1]))
    else:
        run(k_ref.shape[1])


def _mla_attention(q, k, v, with_ctx):
    nb, tt, _ = q.shape
    t_off = 0 if with_ctx else 1
    nsteps = tt // TM - t_off
    return pl.pallas_call(
        functools.partial(_mla_kernel, t_off=t_off, with_ctx=with_ctx),
        grid=(nb, nsteps),
        in_specs=[
            pl.BlockSpec((1, TM, MLA_HEADS * MLA_GROUP), lambda b, i: (b, i + t_off, 0)),
            pl.BlockSpec((1, tt, MLA_HEADS * MLA_GROUP), lambda b, i: (b, 0, 0)),
            pl.BlockSpec((1, tt, MLA_WIDTH), lambda b, i: (b, 0, 0)),
        ],
        out_specs=pl.BlockSpec((1, TM, MLA_WIDTH), lambda b, i: (b, i + t_off, 0)),
        out_shape=jax.ShapeDtypeStruct((nb, tt, MLA_WIDTH), BF16),
        compiler_params=pltpu.CompilerParams(vmem_limit_bytes=VMEM_LIMIT),
        name="mla_attention",
    )(q, k, v)


LRU_PAD = SUBLANES


def _scan_chunk(a, b, carry, reverse):
    nt = a.shape[0] // SUBLANES
    av = a.reshape(nt, SUBLANES, a.shape[1])
    bv = b.reshape(nt, SUBLANES, b.shape[1])
    rid = lax.broadcasted_iota(jnp.int32, av.shape, 1)
    for s in (1, 2, 4):
        if reverse:
            keep = rid < SUBLANES - s
            shift = SUBLANES - s
        else:
            keep = rid >= s
            shift = s
        a_sh = jnp.where(keep, pltpu.roll(av, shift, 1), 1.0)
        b_sh = jnp.where(keep, pltpu.roll(bv, shift, 1), 0.0)
        bv = av * b_sh + bv
        av = av * a_sh
    edge = 0 if reverse else SUBLANES - 1
    hs = [None] * nt
    for k in (reversed(range(nt)) if reverse else range(nt)):
        hk = bv[k] + av[k] * carry
        hs[k] = hk
        carry = hk[edge:edge + 1, :]
    return jnp.concatenate(hs, axis=0), carry


def _lru_kernel(lu_ref, lg_ref, cw_ref, cb_ref, wg_ref, bg_ref, lam_ref, o_ref, pad_ref, u_ref, hs_ref):
    tt = lu_ref.shape[1]
    nchunk = tt // TM
    zeros = jnp.zeros((LRU_PAD, LRU_W), F32)
    pad_ref[0:LRU_PAD, :] = zeros
    pad_ref[LRU_PAD + tt:2 * LRU_PAD + tt, :] = zeros
    pad_ref[LRU_PAD:LRU_PAD + tt, :] = lu_ref[0]

    left = CONV_W // 2
    for c in range(nchunk):
        base = c * TM
        rid = lax.broadcasted_iota(jnp.int32, (TM, LRU_W), 0) + base
        acc = jnp.broadcast_to(cb_ref[...], (TM, LRU_W))
        for j in range(CONV_W):
            off = j - left
            tap = pad_ref[pl.ds(LRU_PAD + base + off, TM), :]
            if off < 0 and base == CTX_LEN:
                tap = jnp.where(rid + off < CTX_LEN, 0.0, tap)
            if off > 0 and base + TM == CTX_LEN:
                tap = jnp.where(rid + off >= CTX_LEN, 0.0, tap)
            acc = acc + tap * cw_ref[j:j + 1, :]
        u_ref[base:base + TM, :] = acc

    for d in (0, 1):
        reverse = d == 1
        nlam = -lam_ref[d]
        sp = jnp.maximum(nlam, 0.0) + jnp.log(1.0 + jnp.exp(-jnp.abs(nlam)))

        def chunk_step(i, carry, d=d, reverse=reverse, sp=sp):
            c = jnp.where(i == 0, 0, nchunk - i) if reverse else i
            rows = pl.ds(pl.multiple_of(c * TM, TM), TM)
            u = u_ref[rows, :]
            g = _dot(u.astype(BF16), wg_ref[d]) + bg_ref[d]
            r = _sigmoid(g[:, :LRU_W])
            ig = _sigmoid(g[:, LRU_W:])
            log_a = -LRU_C * r * sp
            a = jnp.exp(log_a)
            bterm = jnp.sqrt(-jnp.tanh(log_a) * (a * a + 1.0)) * ig * u
            h, carry = _scan_chunk(a, bterm, carry, reverse)
            if reverse:
                hs_ref[rows, :] = hs_ref[rows, :] + h
            else:
                hs_ref[rows, :] = h
            return carry

        lax.fori_loop(0, nchunk, chunk_step, jnp.zeros((1, LRU_W), F32))

    k0 = float(np.sqrt(2.0 / np.pi))
    for c in range(nchunk):
        rows = slice(c * TM, (c + 1) * TM)
        z = lg_ref[0, rows, :]
        gelu = z * (0.5 * (1.0 + jnp.tanh(k0 * (z + 0.044715 * (z * z * z)))))
        o_ref[0, rows, :] = (gelu * hs_ref[rows, :]).astype(BF16)


def _lru(lu, lg, cw, cb, wg, bg, lam):
    nb, tt, _ = lu.shape
    full = pl.BlockSpec((1, tt, LRU_W), lambda b: (b, 0, 0))
    return pl.pallas_call(
        _lru_kernel,
        grid=(nb,),
        in_specs=[full, full, _const_spec(cw.shape), _const_spec(cb.shape), _const_spec(wg.shape),
                  _const_spec(bg.shape), _const_spec(lam.shape)],
        out_specs=full,
        out_shape=jax.ShapeDtypeStruct((nb, tt, LRU_W), BF16),
        scratch_shapes=[pltpu.VMEM((tt + 2 * LRU_PAD, LRU_W), F32), pltpu.VMEM((tt, LRU_W), F32),
                        pltpu.VMEM((tt, LRU_W), F32)],
        compiler_params=pltpu.CompilerParams(vmem_limit_bytes=VMEM_LIMIT),
        name="rglru",
    )(lu, lg, cw, cb, wg, bg, lam)


FF_CHUNK = 1024


def _mix_mlp_kernel(x_ref, ona_ref, omla_ref, olru_ref, gt_ref, mod_ref, gmlp_ref, wna_ref, wmla_ref, wlru_ref,
                    wo_ref, w1_ref, w2_ref, o_ref):
    x = x_ref[0]
    y = _sigmoid(gt_ref[0, :, 0:D_MODEL].astype(F32)) * _dot(ona_ref[0], wna_ref[...])
    y = y + _sigmoid(gt_ref[0, :, D_MODEL:2 * D_MODEL].astype(F32)) * _dot(omla_ref[0], wmla_ref[...])
    y = y + _sigmoid(gt_ref[0, :, 2 * D_MODEL:3 * D_MODEL].astype(F32)) * _dot(olru_ref[0], wlru_ref[...])
    x1 = x + mod_ref[0, 2:3, :] * _dot(y.astype(BF16), wo_ref[...])

    ms = jnp.mean(x1 * x1, axis=-1, keepdims=True)
    h2 = x1 * lax.rsqrt(ms + EPS) * gmlp_ref[...]
    h2 = (h2 * (1.0 + mod_ref[0, 4:5, :]) + mod_ref[0, 3:4, :]).astype(BF16)
    acc = jnp.zeros_like(x1)
    for c in range(D_FF // FF_CHUNK):
        a = jnp.maximum(_dot(h2, w1_ref[:, FF_CHUNK * c:FF_CHUNK * (c + 1)]), 0.0)
        acc = acc + _dot((a * a).astype(BF16), w2_ref[FF_CHUNK * c:FF_CHUNK * (c + 1), :])
    o_ref[0] = x1 + mod_ref[0, 5:6, :] * acc


def _mix_mlp(xs, ona, omla, olru, gt, mod, gmlp, wna, wmla, wlru, wo, w1, w2, with_ctx):
    nb, tt, _ = xs.shape
    t_off = 0 if with_ctx else 1
    nt = tt // TM - t_off
    tok = lambda width: pl.BlockSpec((1, TM, width), lambda b, i: (b, i + t_off, 0))
    mod_idx = _mod_index(nb)
    return pl.pallas_call(
        _mix_mlp_kernel,
        grid=(nb, nt),
        in_specs=[
            tok(D_MODEL), tok(NA_WIDTH), tok(MLA_WIDTH), tok(LRU_W), tok(3 * D_MODEL),
            pl.BlockSpec((1, 6, D_MODEL), lambda b, i: mod_idx(b, i + t_off)),
            _const_spec(gmlp.shape), _const_spec(wna.shape), _const_spec(wmla.shape), _const_spec(wlru.shape),
            _const_spec(wo.shape), _const_spec(w1.shape), _const_spec(w2.shape),
        ],
        out_specs=pl.BlockSpec((1, TM, D_MODEL), lambda b, i: (b, i, 0)),
        out_shape=jax.ShapeDtypeStruct((nb, nt * TM, D_MODEL), F32),
        compiler_params=pltpu.CompilerParams(vmem_limit_bytes=VMEM_LIMIT),
        name="mix_mlp",
    )(xs, ona, omla, olru, gt, mod, gmlp, wna, wmla, wlru, wo, w1, w2)


def _relayout_w_in(w):
    o_mr = 3 * NA_WIDTH + MLA_Q_RANK + MLA_KV_RANK
    zeros = lambda n: jnp.zeros((w.shape[0], n), w.dtype)
    mr_group = jnp.concatenate([zeros(MLA_NOPE), w[:, o_mr:o_mr + MLA_ROPE], zeros(LANES - MLA_QK)], axis=1)
    return jnp.concatenate([w[:, :o_mr], mr_group, w[:, o_mr + MLA_ROPE:]], axis=1).astype(BF16)


def _pad_heads(w, width):
    k = w.shape[0]
    w = w.reshape(k, MLA_HEADS, width)
    return jnp.pad(w, ((0, 0), (0, 0), (0, MLA_GROUP - width))).reshape(k, MLA_HEADS * MLA_GROUP)


def _rope_tables(tt):
    p = np.arange(tt - CTX_LEN)
    row = jnp.asarray(p // GRID_W, F32)
    col = jnp.asarray(p % GRID_W, F32)
    nf = MLA_ROPE // 4
    inv = ROPE_BASE ** (-jnp.arange(nf, dtype=F32) / nf)
    blocks_c, blocks_a, blocks_b = [], [], []
    for pos in (row, col):
        ang = pos[:, None] * inv
        cos, sin = jnp.cos(ang), jnp.sin(ang)
        zero = jnp.zeros_like(sin)
        blocks_c += [cos, cos]
        blocks_a += [-sin, zero]
        blocks_b += [zero, sin]

    def table(blocks, fill):
        lat = jnp.concatenate([jnp.full((tt - CTX_LEN, MLA_NOPE), fill, F32)] + blocks
                              + [jnp.full((tt - CTX_LEN, LANES - MLA_QK), fill, F32)], axis=1)
        return jnp.concatenate([jnp.full((CTX_LEN, LANES), fill, F32), lat], axis=0)

    return table(blocks_c, 1.0), table(blocks_a, 0.0), table(blocks_b, 0.0)


def _block_diag(w):
    eye = jnp.eye(LRU_BLOCKS, dtype=w.dtype)
    return jnp.einsum('ncd,nm->ncmd', w, eye).reshape(LRU_W, LRU_W)


def kernel(x, c, ctx, c_ctx, w_mod, b_mod, g_mix, g_mlp, w_in, na_q_gain, na_k_gain, na_rpb, mla_qa_gain, w_q_b,
           mla_kva_gain, w_kv_b, mla_q_gain, mla_k_gain, lru_conv_w, lru_conv_b, lru_wa, lru_ba, lru_wx, lru_bx,
           lru_lambda, w_na_o, w_mla_o, w_lru_o, w_o, w_ff1, w_ff2):
    nb, seq, _ = x.shape
    depth = w_in.shape[0]
    assert ctx.shape[1] == CTX_LEN and seq % TM == 0 and seq % GRID_W == 0 and seq // GRID_W == GRID_W // 2
    tt = CTX_LEN + seq

    xs = jnp.concatenate([ctx, x], axis=1)
    rp = -(-(nb + 1) // SUBLANES) * SUBLANES
    cvec = jnp.concatenate([c, c_ctx[None, :], jnp.zeros((rp - nb - 1, D_MODEL), F32)], axis=0)
    mod_all = _modulation(cvec, w_mod, b_mod).reshape(depth, rp, 6, D_MODEL)
    rc, ra, rb = _rope_tables(tt)
    row = lambda v: v.reshape(1, -1)
    pad_gain = lambda g: jnp.pad(g, (0, MLA_GROUP - MLA_QK)).reshape(1, MLA_GROUP)

    for i in range(depth):
        with_ctx = i < depth - 1
        kvb = w_kv_b[i].reshape(MLA_KV_RANK, MLA_HEADS, MLA_NOPE + MLA_V)
        wk = _pad_heads(kvb[:, :, :MLA_NOPE].reshape(MLA_KV_RANK, -1), MLA_NOPE).astype(BF16)
        wv = kvb[:, :, MLA_NOPE:].reshape(MLA_KV_RANK, MLA_WIDTH).astype(BF16)
        naq, nak, nav, qm, km, vm, lu, lg, gt = _inproj(
            xs, mod_all[i], row(g_mix[i]), _relayout_w_in(w_in[i]),
            row(jnp.tile(na_q_gain[i], 2)), row(jnp.tile(na_k_gain[i], 2)),
            row(mla_qa_gain[i]), _pad_heads(w_q_b[i], MLA_QK).astype(BF16), row(mla_kva_gain[i]), wk, wv,
            pad_gain(mla_q_gain[i]), pad_gain(mla_k_gain[i]), rc, ra, rb)

        o_na = _na_attention(naq, nak, nav, _na_bias_table(na_rpb[i]), with_ctx)
        o_mla = _mla_attention(qm, km, vm, with_ctx)

        wg = jnp.stack([jnp.concatenate([_block_diag(lru_wa[i, d]), _block_diag(lru_wx[i, d])], axis=1)
                        for d in (0, 1)]).astype(BF16)
        bg = jnp.concatenate([lru_ba[i], lru_bx[i]], axis=1).reshape(2, 1, 2 * LRU_W)
        o_lru = _lru(lu, lg, lru_conv_w[i], row(lru_conv_b[i]), wg, bg, lru_lambda[i].reshape(2, 1, LRU_W))

        xs = _mix_mlp(xs, o_na, o_mla, o_lru, gt, mod_all[i], row(g_mlp[i]),
                      w_na_o[i].astype(BF16), w_mla_o[i].astype(BF16), w_lru_o[i].astype(BF16),
                      w_o[i].astype(BF16), w_ff1[i].astype(BF16), w_ff2[i].astype(BF16), with_ctx)
    return xs
```

```python
import functools

import numpy as np
import jax
import jax.numpy as jnp
from jax import lax
from jax.experimental import pallas as pl
from jax.experimental.pallas import tpu as pltpu

F32 = jnp.float32
BF16 = jnp.bfloat16

D_MODEL = 1024
CTX_LEN = 256
GRID_W = 64
NA_HEADS = 8
NA_DH = 64
NA_KH = 8
NA_KW = 16
NA_WIDTH = NA_HEADS * NA_DH
MLA_HEADS = 8
MLA_NOPE = 64
MLA_ROPE = 32
MLA_V = 64
MLA_QK = MLA_NOPE + MLA_ROPE
MLA_Q_RANK = 384
MLA_KV_RANK = 256
MLA_WIDTH = MLA_HEADS * MLA_V
LRU_W = 512
LRU_BLOCKS = 8
LRU_BS = LRU_W // LRU_BLOCKS
LRU_C = 8.0
CONV_W = 4
D_FF = 4 * D_MODEL
ROPE_BASE = 10000.0
EPS = 1e-6
NEG = -1e30
LOG2E = float(np.log2(np.e))

LANES = 128
SUBLANES = 8
TQ = 256
TM_PROJ = 768
TM_MIX = 384
MLA_GROUP = LANES
VMEM_LIMIT = 56 * 1024 * 1024

C_NAQ, C_NAK, C_NAV = 0, 512, 1024
C_MQ = 1536
C_MKV = C_MQ + MLA_Q_RANK
C_MR = C_MKV + MLA_KV_RANK
C_LU = C_MR + LANES
C_LG = C_LU + LRU_W
C_END = C_LG + LRU_W


def _sigmoid(z):
    return 1.0 / (1.0 + jnp.exp(-z))


def _dot(a, b):
    return jnp.dot(a, b, preferred_element_type=F32)


def _dot_t(a, b):
    return lax.dot_general(a, b, (((1,), (1,)), ((), ())), preferred_element_type=F32)


def _const_spec(shape):
    nd = len(shape)
    return pl.BlockSpec(shape, lambda *_: (0,) * nd, pipeline_mode=pl.Buffered(1))


def _modulated_norm(x, gain, shift, scale):
    ms = jnp.mean(x * x, axis=-1, keepdims=True)
    return (x * lax.rsqrt(ms + EPS) * gain) * (1.0 + scale) + shift


def _mod_rows(modb_ref, modc_ref, is_ctx):
    return lambda k: jnp.where(is_ctx, modc_ref[0, k:k + 1, :], modb_ref[0, k:k + 1, :])


def _mod_kernel(c_ref, w_ref, b_ref, o_ref):
    cv = c_ref[...]
    s = cv * _sigmoid(cv)
    o_ref[0] = _dot(s.astype(BF16), w_ref[0].astype(BF16)) + b_ref[0]


def _modulation(cvec, w_mod, b_mod):
    depth = w_mod.shape[0]
    rp = cvec.shape[0]
    return pl.pallas_call(
        _mod_kernel,
        grid=(depth, 6),
        in_specs=[
            pl.BlockSpec((rp, D_MODEL), lambda i, j: (0, 0)),
            pl.BlockSpec((1, D_MODEL, D_MODEL), lambda i, j: (i, 0, j)),
            pl.BlockSpec((1, 1, D_MODEL), lambda i, j: (i, 0, j)),
        ],
        out_specs=pl.BlockSpec((1, rp, D_MODEL), lambda i, j: (i, 0, j)),
        out_shape=jax.ShapeDtypeStruct((depth, rp, 6 * D_MODEL), F32),
        name="modulation",
    )(cvec, w_mod, b_mod.reshape(depth, 1, 6 * D_MODEL))


def _head_norm64(p, ind, gain):
    ssq = _dot((p * p).astype(BF16), ind)
    return p * lax.rsqrt(ssq * (1.0 / NA_DH) + EPS) * gain


def _rope(n, rc, rs):
    return n * rc + pltpu.roll(n, LANES // 2, 1) * rs


def _inproj_kernel(x_ref, modb_ref, modc_ref, gmix_ref, w_ref, ind_ref, naqg_ref, nakg_ref, qag_ref, wqb_ref,
                   kvag_ref, wk_ref, wv_ref, mqg_ref, mkg_ref, rc_ref, rs_ref,
                   naq_ref, nak_ref, nav_ref, qm_ref, km_ref, vm_ref, lu_ref, lg_ref):
    tm = x_ref.shape[1]
    is_ctx = (lax.broadcasted_iota(jnp.int32, (tm, 1), 0) < CTX_LEN) & (pl.program_id(1) == 0)
    mod = _mod_rows(modb_ref, modc_ref, is_ctx)
    h = _modulated_norm(x_ref[0], gmix_ref[...], mod(0), mod(1)).astype(BF16)

    def proj(a, b):
        return _dot(h, w_ref[:, a:b])

    rc, rs = rc_ref[...], rs_ref[...]

    pq = proj(C_MQ, C_MKV)
    nq = pq * lax.rsqrt(jnp.mean(pq * pq, axis=-1, keepdims=True) + EPS) * qag_ref[...]
    q0 = _dot(nq.astype(BF16), wqb_ref[...])

    pkv = proj(C_MKV, C_MR)
    nkv = (pkv * lax.rsqrt(jnp.mean(pkv * pkv, axis=-1, keepdims=True) + EPS) * kvag_ref[...]).astype(BF16)
    vm_ref[0] = _dot(nkv, wv_ref[...]).astype(BF16)
    k0 = _dot(nkv, wk_ref[...])
    kr = proj(C_MR, C_LU)
    kr_ss = jnp.sum(kr * kr, axis=-1, keepdims=True)
    kr_rot = _rope(kr * mkg_ref[...], rc, rs)

    def q_head(hd):
        sl = slice(MLA_GROUP * hd, MLA_GROUP * (hd + 1))
        blk = q0[:, sl]
        ss = jnp.sum(blk * blk, axis=-1, keepdims=True) * (1.0 / MLA_QK)
        qm_ref[0, :, sl] = _rope(blk * lax.rsqrt(ss + EPS) * mqg_ref[...], rc, rs).astype(BF16)

    def k_head(hd):
        sl = slice(MLA_GROUP * hd, MLA_GROUP * (hd + 1))
        blk = k0[:, sl]
        ss = (jnp.sum(blk * blk, axis=-1, keepdims=True) + kr_ss) * (1.0 / MLA_QK)
        km_ref[0, :, sl] = ((blk * mkg_ref[...] + kr_rot) * lax.rsqrt(ss + EPS)).astype(BF16)

    def wide(i):
        if i == 0:
            naq_ref[0] = _head_norm64(proj(C_NAQ, C_NAK), ind_ref[...], naqg_ref[...]).astype(BF16)
        elif i == 1:
            nak_ref[0] = _head_norm64(proj(C_NAK, C_NAV), ind_ref[...], nakg_ref[...]).astype(BF16)
        elif i == 2:
            nav_ref[0] = proj(C_NAV, C_MQ).astype(BF16)
        elif i == 3:
            lu_ref[0] = proj(C_LU, C_LG).astype(BF16)
        else:
            lg_ref[0] = proj(C_LG, C_END).astype(BF16)

    for i in range(5):
        wide(i)
        for hd in range(2 * i, min(2 * i + 2, MLA_HEADS)):
            q_head(hd)
            k_head(hd)


def _inproj(xs, mod, gmix, w, ind, naqg, nakg, qag, wqb, kvag, wk, wv, mqg, mkg, rc, rs):
    nb, tt, _ = xs.shape
    tm = TM_PROJ
    tok = lambda width: pl.BlockSpec((1, tm, width), lambda b, t: (b, t, 0))
    rope = pl.BlockSpec((tm, LANES), lambda b, t: (t, 0))
    out_widths = (NA_WIDTH, NA_WIDTH, NA_WIDTH, MLA_HEADS * MLA_GROUP, MLA_HEADS * MLA_GROUP, MLA_WIDTH,
                  LRU_W, LRU_W)
    consts = (gmix, w, ind, naqg, nakg, qag, wqb, kvag, wk, wv, mqg, mkg)
    return pl.pallas_call(
        _inproj_kernel,
        grid=(nb, tt // tm),
        in_specs=[
            tok(D_MODEL),
            pl.BlockSpec((1, 6, D_MODEL), lambda b, t: (b, 0, 0)),
            pl.BlockSpec((1, 6, D_MODEL), lambda b, t: (nb, 0, 0)),
            *[_const_spec(a.shape) for a in consts],
            rope, rope,
        ],
        out_specs=[tok(wd) for wd in out_widths],
        out_shape=[jax.ShapeDtypeStruct((nb, tt, wd), BF16) for wd in out_widths],
        compiler_params=pltpu.CompilerParams(vmem_limit_bytes=VMEM_LIMIT),
        name="inproj",
    )(xs, mod, mod, *consts, rc, rs)


NA_WIN = NA_KH * GRID_W
NA_ROWS_PER_STEP = TQ // GRID_W


def _softmax_pv(scores, values):
    m = functools.reduce(jnp.maximum, [jnp.max(s, axis=-1, keepdims=True) for s in scores])
    ps = [jnp.exp2(s - m) for s in scores]
    l = functools.reduce(jnp.add, [jnp.sum(p, axis=-1, keepdims=True) for p in ps])
    o = functools.reduce(jnp.add, [_dot(p.astype(BF16), v) for p, v in zip(ps, values)])
    return o * (1.0 / l)


def _stack_heads(q):
    lo = lax.broadcasted_iota(jnp.int32, q.shape, 1) < NA_DH
    return jnp.concatenate([jnp.where(lo, q, 0), jnp.where(lo, 0, q)], axis=0)


def _unstack_heads(o):
    n = o.shape[0] // 2
    lo = lax.broadcasted_iota(jnp.int32, (n, LANES), 1) < NA_DH
    return jnp.where(lo, o[:n], o[n:])


def _na_kernel(q_ref, k_ref, v_ref, bias_ref, o_ref, *, t_off, with_ctx):
    t = pl.program_id(1) + t_off
    npair = NA_WIDTH // LANES

    def ctx_block():
        outs = []
        for j in range(npair):
            cols = slice(LANES * j, LANES * (j + 1))
            s_c = _dot_t(_stack_heads(q_ref[0, :, cols]), k_ref[0, 0:CTX_LEN, cols])
            outs.append(_unstack_heads(_softmax_pv([s_c], [v_ref[0, 0:CTX_LEN, cols]])))
        o_ref[0] = jnp.concatenate(outs, axis=-1).astype(BF16)

    def latent_rows():
        ng = NA_ROWS_PER_STEP
        sq = 2 * GRID_W
        offs, wins = [], []
        for g in range(ng):
            r = (t - 1) * ng + g
            rs = jnp.clip(r - NA_KH // 2, 0, GRID_W // 2 - NA_KH)
            offs.append(rs - r + NA_KH - 1)
            wins.append(pl.ds(pl.multiple_of(CTX_LEN + rs * GRID_W, GRID_W), NA_WIN))
        lo = lax.broadcasted_iota(jnp.int32, (TQ, LANES), 1) < NA_DH
        grp = lambda z, g: z[sq * g:sq * (g + 1)]
        def scores(j):
            cols = slice(LANES * j, LANES * (j + 1))
            q = q_ref[0, :, cols]
            q_lo, q_hi = jnp.where(lo, q, 0), jnp.where(lo, 0, q)
            qst = jnp.concatenate([part[GRID_W * g:GRID_W * (g + 1)] for g in range(ng) for part in (q_lo, q_hi)],
                                  axis=0)
            s_c = _dot_t(qst, k_ref[0, 0:CTX_LEN, cols])
            s_w = [_dot_t(grp(qst, g), k_ref[0, wins[g], cols])
                   + bias_ref[offs[g], 2 * j:2 * j + 2].reshape(sq, NA_WIN) for g in range(ng)]
            return s_c, s_w

        def numerators(s_c, s_w):
            pcs, pws, ls = [], [], []
            for g in range(ng):
                sc, sw = grp(s_c, g), s_w[g]
                m = jnp.maximum(jnp.max(sc, axis=-1, keepdims=True), jnp.max(sw, axis=-1, keepdims=True))
                pc, pw = jnp.exp2(sc - m), jnp.exp2(sw - m)
                ls.append(1.0 / (jnp.sum(pc, axis=-1, keepdims=True) + jnp.sum(pw, axis=-1, keepdims=True)))
                pcs.append(pc.astype(BF16))
                pws.append(pw.astype(BF16))
            return jnp.concatenate(pcs, axis=0), pws, ls

        def values(j, p_c, p_w, inv_l):
            cols = slice(LANES * j, LANES * (j + 1))
            o_c = _dot(p_c, v_ref[0, 0:CTX_LEN, cols])
            rows = [_unstack_heads((grp(o_c, g) + _dot(p_w[g], v_ref[0, wins[g], cols])) * inv_l[g])
                    for g in range(ng)]
            return jnp.concatenate(rows, axis=0)

        outs = []
        s_next = scores(0)
        for j in range(npair):
            s_cur = s_next
            if j + 1 < npair:
                s_next = scores(j + 1)
            outs.append(values(j, *numerators(*s_cur)))
        o_ref[0] = jnp.concatenate(outs, axis=-1).astype(BF16)

    if with_ctx:
        pl.when(t == 0)(ctx_block)
        pl.when(t > 0)(latent_rows)
    else:
        latent_rows()


def _na_attention(q, k, v, bias, with_ctx):
    nb, tt, _ = q.shape
    t_off = 0 if with_ctx else 1
    full = pl.BlockSpec((1, tt, NA_WIDTH), lambda b, i: (b, 0, 0))
    blk = pl.BlockSpec((1, TQ, NA_WIDTH), lambda b, i: (b, i + t_off, 0))
    return pl.pallas_call(
        functools.partial(_na_kernel, t_off=t_off, with_ctx=with_ctx),
        grid=(nb, tt // TQ - t_off),
        in_specs=[blk, full, full, _const_spec(bias.shape)],
        out_specs=blk,
        out_shape=jax.ShapeDtypeStruct((nb, tt, NA_WIDTH), BF16),
        compiler_params=pltpu.CompilerParams(vmem_limit_bytes=VMEM_LIMIT),
        name="na_attention",
    )(q, k, v, bias)


def _na_bias_table(rpb):
    qc = np.arange(GRID_W)[:, None]
    kc = np.arange(GRID_W)[None, :]
    cs = np.clip(qc - NA_KW // 2, 0, GRID_W - NA_KW)
    valid = (kc >= cs) & (kc < cs + NA_KW)
    dc = np.clip(kc - qc + NA_KW - 1, 0, 2 * NA_KW - 2)
    dr = np.arange(NA_KH)[:, None] + np.arange(NA_KH)[None, :]
    tab = rpb[:, dr][:, :, :, dc] * LOG2E
    tab = jnp.where(valid[None, None, None], tab, NEG)
    tab = jnp.transpose(tab, (1, 0, 3, 2, 4))
    return tab.reshape(NA_KH, NA_HEADS, GRID_W, NA_WIN).astype(F32)


def _mla_kernel(q_ref, k_ref, v_ref, o_ref, *, t_off, with_ctx):
    t = pl.program_id(1) + t_off
    lo = lax.broadcasted_iota(jnp.int32, (TQ, LANES), 1) < MLA_V

    def run(nk):
        def scores(hd):
            hs = slice(MLA_GROUP * hd, MLA_GROUP * (hd + 1))
            return _dot_t(q_ref[0, :, hs], k_ref[0, 0:nk, hs])

        outs = []
        s_next = scores(0)
        for hd in range(MLA_HEADS):
            s = s_next
            if hd + 1 < MLA_HEADS:
                s_next = scores(hd + 1)
            vcols = slice(LANES * (hd // 2), LANES * (hd // 2 + 1))
            outs.append(_softmax_pv([s], [v_ref[0, 0:nk, vcols]]))
        pairs = [jnp.where(lo, outs[2 * j], outs[2 * j + 1]) for j in range(MLA_WIDTH // LANES)]
        o_ref[0] = jnp.concatenate(pairs, axis=-1).astype(BF16)

    if with_ctx:
        pl.when(t == 0)(lambda: run(CTX_LEN))
        pl.when(t > 0)(lambda: run(k_ref.shape[1]))
    else:
        run(k_ref.shape[1])


def _mla_attention(q, k, v, with_ctx):
    nb, tt, _ = q.shape
    t_off = 0 if with_ctx else 1
    return pl.pallas_call(
        functools.partial(_mla_kernel, t_off=t_off, with_ctx=with_ctx),
        grid=(nb, tt // TQ - t_off),
        in_specs=[
            pl.BlockSpec((1, TQ, MLA_HEADS * MLA_GROUP), lambda b, i: (b, i + t_off, 0)),
            pl.BlockSpec((1, tt, MLA_HEADS * MLA_GROUP), lambda b, i: (b, 0, 0)),
            pl.BlockSpec((1, tt, MLA_WIDTH), lambda b, i: (b, 0, 0)),
        ],
        out_specs=pl.BlockSpec((1, TQ, MLA_WIDTH), lambda b, i: (b, i + t_off, 0)),
        out_shape=jax.ShapeDtypeStruct((nb, tt, MLA_WIDTH), BF16),
        compiler_params=pltpu.CompilerParams(vmem_limit_bytes=VMEM_LIMIT),
        name="mla_attention",
    )(q, k, v)


LRU_CHUNK = 256
LRU_TILES = LRU_CHUNK // SUBLANES


def _chunk_permutation():
    i = np.arange(LRU_CHUNK)
    p = np.zeros((LRU_CHUNK, LRU_CHUNK), np.float32)
    p[i, LRU_TILES * (i % SUBLANES) + i // SUBLANES] = 1.0
    return p


def _scan_tiles(a, b, carry, reverse):
    order = list(reversed(range(LRU_TILES))) if reverse else list(range(LRU_TILES))
    acum, bcum = [None] * LRU_TILES, [None] * LRU_TILES
    prev = None
    for k in order:
        if prev is None:
            acum[k], bcum[k] = a[k], b[k]
        else:
            acum[k] = a[k] * acum[prev]
            bcum[k] = a[k] * bcum[prev] + b[k]
        prev = k
    p_end, e_end = acum[prev], bcum[prev]
    sub = list(reversed(range(SUBLANES))) if reverse else list(range(SUBLANES))
    cin = [None] * SUBLANES
    for s in sub:
        cin[s] = carry
        carry = p_end[s:s + 1, :] * carry + e_end[s:s + 1, :]
    cin = jnp.concatenate(cin, axis=0)
    return [bcum[k] + acum[k] * cin for k in range(LRU_TILES)], carry


def _lru_kernel(lu_ref, lg_ref, perm_ref, permt_ref, cw_ref, cb_ref, wg_ref, bg_ref, lam_ref, o_ref,
                lp_ref, u_ref, hs_ref):
    tt = lu_ref.shape[1]
    nchunk = tt // LRU_CHUNK
    chunk_rows = lambda c: slice(c * LRU_CHUNK, (c + 1) * LRU_CHUNK)
    for c in range(nchunk):
        lp_ref[chunk_rows(c), :] = _dot(perm_ref[...], lu_ref[0, chunk_rows(c), :])

    left = CONV_W // 2
    sub = lax.broadcasted_iota(jnp.int32, (SUBLANES, LRU_W), 0)

    def conv_chunk(c, _):
        base = pl.multiple_of(c * LRU_CHUNK, LRU_CHUNK)
        tile = lambda b, k: lp_ref[pl.ds(b + SUBLANES * k, SUBLANES), :]
        prev_base = pl.multiple_of(jnp.maximum(c - 1, 0) * LRU_CHUNK, LRU_CHUNK)
        next_base = pl.multiple_of(jnp.minimum(c + 1, nchunk - 1) * LRU_CHUNK, LRU_CHUNK)
        has_prev = c >= 2
        has_next = (c >= 1) & (c < nchunk - 1)

        def earlier(k):
            edge = jnp.where(has_prev, tile(prev_base, k), 0.0)
            return pltpu.roll(jnp.where(sub == SUBLANES - 1, edge, tile(base, k)), 1, 0)

        def later(k):
            edge = jnp.where(has_next, tile(next_base, k), 0.0)
            return pltpu.roll(jnp.where(sub == 0, edge, tile(base, k)), SUBLANES - 1, 0)

        for k in range(LRU_TILES):
            acc = jnp.broadcast_to(cb_ref[...], (SUBLANES, LRU_W))
            for j in range(CONV_W):
                kk = k + j - left
                if kk < 0:
                    tap = earlier(kk + LRU_TILES)
                elif kk >= LRU_TILES:
                    tap = later(kk - LRU_TILES)
                else:
                    tap = tile(base, kk)
                acc = acc + tap * cw_ref[j:j + 1, :]
            u_ref[pl.ds(base + SUBLANES * k, SUBLANES), :] = acc
        return 0

    lax.fori_loop(0, nchunk, conv_chunk, 0)

    for d in (0, 1):
        reverse = d == 1
        nlam = -lam_ref[d]
        sp = jnp.maximum(nlam, 0.0) + jnp.log(1.0 + jnp.exp(-jnp.abs(nlam)))

        def chunk_step(i, carry, d=d, reverse=reverse, sp=sp):
            c = jnp.where(i == 0, 0, nchunk - i) if reverse else i
            rows = pl.ds(pl.multiple_of(c * LRU_CHUNK, LRU_CHUNK), LRU_CHUNK)
            u = u_ref[rows, :]
            g = _dot(u.astype(BF16), wg_ref[d]) + bg_ref[d]
            r = _sigmoid(g[:, :LRU_W])
            ig = _sigmoid(g[:, LRU_W:])
            log_a = -LRU_C * r * sp
            a = jnp.exp(log_a)
            bterm = jnp.sqrt(-jnp.tanh(log_a) * (a * a + 1.0)) * ig * u
            split = lambda z: [z[SUBLANES * k:SUBLANES * (k + 1), :] for k in range(LRU_TILES)]
            h, carry = _scan_tiles(split(a), split(bterm), carry, reverse)
            h = jnp.concatenate(h, axis=0)
            hs_ref[rows, :] = hs_ref[rows, :] + h if reverse else h
            return carry

        lax.fori_loop(0, nchunk, chunk_step, jnp.zeros((1, LRU_W), F32))

    k0 = float(np.sqrt(2.0 / np.pi))
    for c in range(nchunk):
        z = _dot(perm_ref[...], lg_ref[0, chunk_rows(c), :])
        gelu = z * (0.5 * (1.0 + jnp.tanh(k0 * (z + 0.044715 * (z * z * z)))))
        o_tile_order = (gelu * hs_ref[chunk_rows(c), :]).astype(BF16)
        o_ref[0, chunk_rows(c), :] = _dot(permt_ref[...], o_tile_order).astype(BF16)


def _lru(lu, lg, cw, cb, wg, bg, lam):
    nb, tt, _ = lu.shape
    full = pl.BlockSpec((1, tt, LRU_W), lambda b: (b, 0, 0))
    perm = _chunk_permutation()
    consts = (jnp.asarray(perm, BF16), jnp.asarray(perm.T, BF16), cw, cb, wg, bg, lam)
    return pl.pallas_call(
        _lru_kernel,
        grid=(nb,),
        in_specs=[full, full, *[_const_spec(a.shape) for a in consts]],
        out_specs=full,
        out_shape=jax.ShapeDtypeStruct((nb, tt, LRU_W), BF16),
        scratch_shapes=[pltpu.VMEM((tt, LRU_W), F32)] * 3,
        compiler_params=pltpu.CompilerParams(vmem_limit_bytes=VMEM_LIMIT),
        name="rglru",
    )(lu, lg, *consts)


FF_CHUNK = 1024


def _mix_mlp_kernel(x_ref, ona_ref, omla_ref, olru_ref, modb_ref, modc_ref, gmix_ref, gmlp_ref, wgt_ref, wna_ref,
                    wmla_ref, wlru_ref, wo_ref, w1_ref, w2_ref, o_ref, *, t_off):
    tm = x_ref.shape[1]
    is_ctx = (lax.broadcasted_iota(jnp.int32, (tm, 1), 0) < CTX_LEN) & (pl.program_id(1) + t_off == 0)
    mod = _mod_rows(modb_ref, modc_ref, is_ctx)
    x = x_ref[0]
    h = _modulated_norm(x, gmix_ref[...], mod(0), mod(1)).astype(BF16)
    y = None
    for g, (o_r, w_r) in enumerate(((ona_ref, wna_ref), (omla_ref, wmla_ref), (olru_ref, wlru_ref))):
        gate = _sigmoid(_dot(h, wgt_ref[:, D_MODEL * g:D_MODEL * (g + 1)]))
        term = gate * _dot(o_r[0], w_r[...])
        y = term if y is None else y + term
    x1 = x + mod(2) * _dot(y.astype(BF16), wo_ref[...])

    h2 = _modulated_norm(x1, gmlp_ref[...], mod(3), mod(4)).astype(BF16)
    acc = jnp.zeros_like(x1)
    for c in range(D_FF // FF_CHUNK):
        a = jnp.maximum(_dot(h2, w1_ref[:, FF_CHUNK * c:FF_CHUNK * (c + 1)]), 0.0)
        acc = acc + _dot((a * a).astype(BF16), w2_ref[FF_CHUNK * c:FF_CHUNK * (c + 1), :])
    o_ref[0] = x1 + mod(5) * acc


def _mix_mlp(xs, ona, omla, olru, mod, gmix, gmlp, wgt, wna, wmla, wlru, wo, w1, w2, with_ctx):
    nb, tt, _ = xs.shape
    tm = TM_MIX if with_ctx else TQ
    t_off = 0 if with_ctx else CTX_LEN // tm
    nt = tt // tm - t_off
    tok = lambda width: pl.BlockSpec((1, tm, width), lambda b, i: (b, i + t_off, 0))
    return pl.pallas_call(
        functools.partial(_mix_mlp_kernel, t_off=t_off),
        grid=(nb, nt),
        in_specs=[
            tok(D_MODEL), tok(NA_WIDTH), tok(MLA_WIDTH), tok(LRU_W),
            pl.BlockSpec((1, 6, D_MODEL), lambda b, i: (b, 0, 0)),
            pl.BlockSpec((1, 6, D_MODEL), lambda b, i: (nb, 0, 0)),
            _const_spec(gmix.shape), _const_spec(gmlp.shape), _const_spec(wgt.shape), _const_spec(wna.shape),
            _const_spec(wmla.shape), _const_spec(wlru.shape), _const_spec(wo.shape), _const_spec(w1.shape),
            _const_spec(w2.shape),
        ],
        out_specs=pl.BlockSpec((1, tm, D_MODEL), lambda b, i: (b, i, 0)),
        out_shape=jax.ShapeDtypeStruct((nb, nt * tm, D_MODEL), F32),
        compiler_params=pltpu.CompilerParams(vmem_limit_bytes=VMEM_LIMIT),
        name="mix_mlp",
    )(xs, ona, omla, olru, mod, mod, gmix, gmlp, wgt, wna, wmla, wlru, wo, w1, w2)


def _group_source():
    nf = MLA_ROPE // 4
    half = LANES // 2
    src = np.full(LANES, -1)
    for part in (0, 1):
        lane0 = half * part
        src[lane0:lane0 + MLA_NOPE // 2] = MLA_NOPE // 2 * part + np.arange(MLA_NOPE // 2)
        src[lane0 + MLA_NOPE // 2:lane0 + MLA_NOPE // 2 + nf] = MLA_NOPE + nf * part + np.arange(nf)
        src[lane0 + MLA_NOPE // 2 + nf:lane0 + MLA_NOPE // 2 + 2 * nf] = MLA_NOPE + 2 * nf + nf * part + np.arange(nf)
    return src


_GROUP_SRC = _group_source()


def _to_group(v):
    return jnp.where(_GROUP_SRC >= 0, jnp.take(v, np.maximum(_GROUP_SRC, 0), axis=-1), 0)


def _heads_to_groups(w, width):
    k = w.shape[0]
    w = jnp.pad(w.reshape(k, MLA_HEADS, width), ((0, 0), (0, 0), (0, MLA_QK - width)))
    return _to_group(w).reshape(k, MLA_HEADS * MLA_GROUP)


def _relayout_w_in(w):
    o_mr = 3 * NA_WIDTH + MLA_Q_RANK + MLA_KV_RANK
    o_gt = o_mr + MLA_ROPE + 2 * LRU_W
    mr_group = _to_group(jnp.pad(w[:, o_mr:o_mr + MLA_ROPE], ((0, 0), (MLA_NOPE, 0))))
    w_proj = jnp.concatenate([w[:, :o_mr], mr_group, w[:, o_mr + MLA_ROPE:o_gt]], axis=1)
    return w_proj.astype(BF16), w[:, o_gt:].astype(BF16)


def _rope_tables(tt):
    p = np.arange(tt - CTX_LEN)
    nf = MLA_ROPE // 4
    inv = ROPE_BASE ** (-jnp.arange(nf, dtype=F32) / nf)
    cos, sin = [], []
    for pos in (jnp.asarray(p // GRID_W, F32), jnp.asarray(p % GRID_W, F32)):
        ang = pos[:, None] * inv
        cos += [jnp.cos(ang), jnp.cos(ang)]
        sin += [-jnp.sin(ang), jnp.sin(ang)]
    ones = jnp.ones((tt - CTX_LEN, MLA_NOPE), F32)
    rc = jnp.concatenate([jnp.ones((CTX_LEN, MLA_QK), F32), jnp.concatenate([ones] + cos, axis=1)], axis=0)
    rs = jnp.concatenate([jnp.zeros((CTX_LEN, MLA_QK), F32), jnp.concatenate([0.0 * ones] + sin, axis=1)], axis=0)
    return _to_group(rc), _to_group(rs)


def _block_diag(w):
    eye = jnp.eye(LRU_BLOCKS, dtype=w.dtype)
    return jnp.einsum('ncd,nm->ncmd', w, eye).reshape(LRU_W, LRU_W)


def kernel(x, c, ctx, c_ctx, w_mod, b_mod, g_mix, g_mlp, w_in, na_q_gain, na_k_gain, na_rpb, mla_qa_gain, w_q_b,
           mla_kva_gain, w_kv_b, mla_q_gain, mla_k_gain, lru_conv_w, lru_conv_b, lru_wa, lru_ba, lru_wx, lru_bx,
           lru_lambda, w_na_o, w_mla_o, w_lru_o, w_o, w_ff1, w_ff2):
    nb, seq, _ = x.shape
    depth = w_in.shape[0]
    assert ctx.shape[1] == CTX_LEN and seq // GRID_W == GRID_W // 2 and seq % GRID_W == 0
    tt = CTX_LEN + seq
    assert tt % TM_PROJ == 0 and tt % TM_MIX == 0 and tt % TQ == 0 and tt % LRU_CHUNK == 0

    xs = jnp.concatenate([ctx, x], axis=1)
    rp = -(-(nb + 1) // SUBLANES) * SUBLANES
    cvec = jnp.concatenate([c, c_ctx[None, :], jnp.zeros((rp - nb - 1, D_MODEL), F32)], axis=0)
    mod_all = _modulation(cvec, w_mod, b_mod).reshape(depth, rp, 6, D_MODEL)
    rc, rs = _rope_tables(tt)
    row = lambda v: v.reshape(1, -1)
    head_ind = jnp.asarray(np.kron(np.eye(NA_HEADS), np.ones((NA_DH, NA_DH))), BF16)

    for i in range(depth):
        with_ctx = i < depth - 1
        kvb = w_kv_b[i].reshape(MLA_KV_RANK, MLA_HEADS, MLA_NOPE + MLA_V)
        wk = _heads_to_groups(kvb[:, :, :MLA_NOPE].reshape(MLA_KV_RANK, -1), MLA_NOPE).astype(BF16)
        wv = kvb[:, :, MLA_NOPE:].reshape(MLA_KV_RANK, MLA_WIDTH).astype(BF16)
        w_proj, w_gate = _relayout_w_in(w_in[i])
        naq, nak, nav, qm, km, vm, lu, lg = _inproj(
            xs, mod_all[i], row(g_mix[i]), w_proj, head_ind,
            row(jnp.tile(na_q_gain[i], NA_HEADS) * (NA_DH ** -0.5 * LOG2E)), row(jnp.tile(na_k_gain[i], NA_HEADS)),
            row(mla_qa_gain[i]), _heads_to_groups(w_q_b[i], MLA_QK).astype(BF16), row(mla_kva_gain[i]), wk, wv,
            row(_to_group(mla_q_gain[i]) * (MLA_QK ** -0.5 * LOG2E)), row(_to_group(mla_k_gain[i])), rc, rs)

        o_na = _na_attention(naq, nak, nav, _na_bias_table(na_rpb[i]), with_ctx)
        o_mla = _mla_attention(qm, km, vm, with_ctx)

        wg = jnp.stack([jnp.concatenate([_block_diag(lru_wa[i, d]), _block_diag(lru_wx[i, d])], axis=1)
                        for d in (0, 1)]).astype(BF16)
        bg = jnp.concatenate([lru_ba[i], lru_bx[i]], axis=1).reshape(2, 1, 2 * LRU_W)
        o_lru = _lru(lu, lg, lru_conv_w[i], row(lru_conv_b[i]), wg, bg, lru_lambda[i].reshape(2, 1, LRU_W))

        xs = _mix_mlp(xs, o_na, o_mla, o_lru, mod_all[i], row(g_mix[i]), row(g_mlp[i]), w_gate,
                      w_na_o[i].astype(BF16), w_mla_o[i].astype(BF16), w_lru_o[i].astype(BF16),
                      w_o[i].astype(BF16), w_ff1[i].astype(BF16), w_ff2[i].astype(BF16), with_ctx)
    return xs
```

```python
import functools

import numpy as np
import jax
import jax.numpy as jnp
from jax import lax
from jax.experimental import pallas as pl
from jax.experimental.pallas import tpu as pltpu

F32 = jnp.float32
BF16 = jnp.bfloat16

D_MODEL = 1024
CTX_LEN = 256
GRID_W = 64
NA_HEADS = 8
NA_DH = 64
NA_KH = 8
NA_KW = 16
NA_WIDTH = NA_HEADS * NA_DH
MLA_HEADS = 8
MLA_NOPE = 64
MLA_ROPE = 32
MLA_V = 64
MLA_QK = MLA_NOPE + MLA_ROPE
MLA_Q_RANK = 384
MLA_KV_RANK = 256
MLA_WIDTH = MLA_HEADS * MLA_V
LRU_W = 512
LRU_BLOCKS = 8
LRU_BS = LRU_W // LRU_BLOCKS
LRU_C = 8.0
CONV_W = 4
D_FF = 4 * D_MODEL
ROPE_BASE = 10000.0
EPS = 1e-6
NEG = -1e30
LOG2E = float(np.log2(np.e))

LANES = 128
SUBLANES = 8
TQ = 256
TM_PROJ = 768
TM_MIX = 384
MLA_GROUP = LANES
VMEM_LIMIT = 56 * 1024 * 1024

C_NAQ, C_NAK, C_NAV = 0, 512, 1024
C_MQ = 1536
C_MKV = C_MQ + MLA_Q_RANK
C_MR = C_MKV + MLA_KV_RANK
C_LU = C_MR + LANES
C_LG = C_LU + LRU_W
C_END = C_LG + LRU_W


def _sigmoid(z):
    return 0.5 * jnp.tanh(0.5 * z) + 0.5


def _dot(a, b):
    return jnp.dot(a, b, preferred_element_type=F32)


def _dot_t(a, b):
    return lax.dot_general(a, b, (((1,), (1,)), ((), ())), preferred_element_type=F32)


def _const_spec(shape):
    nd = len(shape)
    return pl.BlockSpec(shape, lambda *_: (0,) * nd, pipeline_mode=pl.Buffered(1))


def _modulated_norm(x, gain, shift, scale):
    ms = jnp.mean(x * x, axis=-1, keepdims=True)
    return (x * lax.rsqrt(ms + EPS) * gain) * (1.0 + scale) + shift


def _mod_rows(modb_ref, modc_ref, is_ctx):
    return lambda k: jnp.where(is_ctx, modc_ref[0, k:k + 1, :], modb_ref[0, k:k + 1, :])


def _mod_kernel(c_ref, w_ref, b_ref, o_ref):
    cv = c_ref[...]
    s = cv * _sigmoid(cv)
    o_ref[0] = _dot(s.astype(BF16), w_ref[0].astype(BF16)) + b_ref[0]


def _modulation(cvec, w_mod, b_mod):
    depth = w_mod.shape[0]
    rp = cvec.shape[0]
    return pl.pallas_call(
        _mod_kernel,
        grid=(depth, 6),
        in_specs=[
            pl.BlockSpec((rp, D_MODEL), lambda i, j: (0, 0)),
            pl.BlockSpec((1, D_MODEL, D_MODEL), lambda i, j: (i, 0, j)),
            pl.BlockSpec((1, 1, D_MODEL), lambda i, j: (i, 0, j)),
        ],
        out_specs=pl.BlockSpec((1, rp, D_MODEL), lambda i, j: (i, 0, j)),
        out_shape=jax.ShapeDtypeStruct((depth, rp, 6 * D_MODEL), F32),
        name="modulation",
    )(cvec, w_mod, b_mod.reshape(depth, 1, 6 * D_MODEL))


def _head_norm64(p, ind, gain):
    ssq = _dot((p * p).astype(BF16), ind)
    return p * lax.rsqrt(ssq * (1.0 / NA_DH) + EPS) * gain


def _rope(n, rc, rs):
    return n * rc + pltpu.roll(n, LANES // 2, 1) * rs


def _inproj_kernel(x_ref, modb_ref, modc_ref, gmix_ref, w_ref, ind_ref, naqg_ref, nakg_ref, qag_ref, wqb_ref,
                   kvag_ref, wk_ref, wv_ref, mqg_ref, mkg_ref, rc_ref, rs_ref,
                   naq_ref, nak_ref, nav_ref, qm_ref, km_ref, vm_ref, lu_ref, lg_ref):
    tm = x_ref.shape[1]
    is_ctx = (lax.broadcasted_iota(jnp.int32, (tm, 1), 0) < CTX_LEN) & (pl.program_id(1) == 0)
    mod = _mod_rows(modb_ref, modc_ref, is_ctx)
    h = _modulated_norm(x_ref[0], gmix_ref[...], mod(0), mod(1)).astype(BF16)

    def proj(a, b):
        return _dot(h, w_ref[:, a:b])

    rc, rs = rc_ref[...], rs_ref[...]

    pq = proj(C_MQ, C_MKV)
    nq = pq * lax.rsqrt(jnp.mean(pq * pq, axis=-1, keepdims=True) + EPS) * qag_ref[...]
    q0 = _dot(nq.astype(BF16), wqb_ref[...])

    pkv = proj(C_MKV, C_MR)
    nkv = (pkv * lax.rsqrt(jnp.mean(pkv * pkv, axis=-1, keepdims=True) + EPS) * kvag_ref[...]).astype(BF16)
    vm_ref[0] = _dot(nkv, wv_ref[...]).astype(BF16)
    k0 = _dot(nkv, wk_ref[...])
    kr = proj(C_MR, C_LU)
    kr_ss = jnp.sum(kr * kr, axis=-1, keepdims=True)
    kr_rot = _rope(kr * mkg_ref[...], rc, rs)

    def q_head(hd):
        sl = slice(MLA_GROUP * hd, MLA_GROUP * (hd + 1))
        blk = q0[:, sl]
        ss = jnp.sum(blk * blk, axis=-1, keepdims=True) * (1.0 / MLA_QK)
        qm_ref[0, :, sl] = _rope(blk * lax.rsqrt(ss + EPS) * mqg_ref[...], rc, rs).astype(BF16)

    def k_head(hd):
        sl = slice(MLA_GROUP * hd, MLA_GROUP * (hd + 1))
        blk = k0[:, sl]
        ss = (jnp.sum(blk * blk, axis=-1, keepdims=True) + kr_ss) * (1.0 / MLA_QK)
        km_ref[0, :, sl] = ((blk * mkg_ref[...] + kr_rot) * lax.rsqrt(ss + EPS)).astype(BF16)

    def wide(i):
        if i == 0:
            naq_ref[0] = _head_norm64(proj(C_NAQ, C_NAK), ind_ref[...], naqg_ref[...]).astype(BF16)
        elif i == 1:
            nak_ref[0] = _head_norm64(proj(C_NAK, C_NAV), ind_ref[...], nakg_ref[...]).astype(BF16)
        elif i == 2:
            nav_ref[0] = proj(C_NAV, C_MQ).astype(BF16)
        elif i == 3:
            lu_ref[0] = proj(C_LU, C_LG).astype(BF16)
        else:
            lg_ref[0] = proj(C_LG, C_END).astype(BF16)

    for i in range(5):
        wide(i)
        for hd in range(2 * i, min(2 * i + 2, MLA_HEADS)):
            q_head(hd)
            k_head(hd)


def _inproj(xs, mod, gmix, w, ind, naqg, nakg, qag, wqb, kvag, wk, wv, mqg, mkg, rc, rs):
    nb, tt, _ = xs.shape
    tm = TM_PROJ
    tok = lambda width: pl.BlockSpec((1, tm, width), lambda b, t: (b, t, 0))
    rope = pl.BlockSpec((tm, LANES), lambda b, t: (t, 0))
    out_widths = (NA_WIDTH, NA_WIDTH, NA_WIDTH, MLA_HEADS * MLA_GROUP, MLA_HEADS * MLA_GROUP, MLA_WIDTH,
                  LRU_W, LRU_W)
    consts = (gmix, w, ind, naqg, nakg, qag, wqb, kvag, wk, wv, mqg, mkg)
    return pl.pallas_call(
        _inproj_kernel,
        grid=(nb, tt // tm),
        in_specs=[
            tok(D_MODEL),
            pl.BlockSpec((1, 6, D_MODEL), lambda b, t: (b, 0, 0)),
            pl.BlockSpec((1, 6, D_MODEL), lambda b, t: (nb, 0, 0)),
            *[_const_spec(a.shape) for a in consts],
            rope, rope,
        ],
        out_specs=[tok(wd) for wd in out_widths],
        out_shape=[jax.ShapeDtypeStruct((nb, tt, wd), BF16) for wd in out_widths],
        compiler_params=pltpu.CompilerParams(vmem_limit_bytes=VMEM_LIMIT),
        name="inproj",
    )(xs, mod, mod, *consts, rc, rs)


NA_WIN = NA_KH * GRID_W
NA_ROWS_PER_STEP = TQ // GRID_W


def _softmax_pv(scores, values):
    m = functools.reduce(jnp.maximum, [jnp.max(s, axis=-1, keepdims=True) for s in scores])
    ps = [jnp.exp2(s - m) for s in scores]
    l = functools.reduce(jnp.add, [jnp.sum(p, axis=-1, keepdims=True) for p in ps])
    o = functools.reduce(jnp.add, [_dot(p.astype(BF16), v) for p, v in zip(ps, values)])
    return o * (1.0 / l)


def _stack_heads(q):
    lo = lax.broadcasted_iota(jnp.int32, q.shape, 1) < NA_DH
    return jnp.concatenate([jnp.where(lo, q, 0), jnp.where(lo, 0, q)], axis=0)


def _unstack_heads(o):
    n = o.shape[0] // 2
    lo = lax.broadcasted_iota(jnp.int32, (n, LANES), 1) < NA_DH
    return jnp.where(lo, o[:n], o[n:])


def _na_kernel(q_ref, k_ref, v_ref, bias_ref, o_ref, *, t_off, with_ctx):
    t = pl.program_id(1) + t_off
    npair = NA_WIDTH // LANES

    def ctx_block():
        outs = []
        for j in range(npair):
            cols = slice(LANES * j, LANES * (j + 1))
            s_c = _dot_t(_stack_heads(q_ref[0, :, cols]), k_ref[0, 0:CTX_LEN, cols])
            outs.append(_unstack_heads(_softmax_pv([s_c], [v_ref[0, 0:CTX_LEN, cols]])))
        o_ref[0] = jnp.concatenate(outs, axis=-1).astype(BF16)

    def latent_rows():
        ng = NA_ROWS_PER_STEP
        sq = 2 * GRID_W
        offs, wins = [], []
        for g in range(ng):
            r = (t - 1) * ng + g
            rs = jnp.clip(r - NA_KH // 2, 0, GRID_W // 2 - NA_KH)
            offs.append(rs - r + NA_KH - 1)
            wins.append(pl.ds(pl.multiple_of(CTX_LEN + rs * GRID_W, GRID_W), NA_WIN))
        lo = lax.broadcasted_iota(jnp.int32, (TQ, LANES), 1) < NA_DH
        grp = lambda z, g: z[sq * g:sq * (g + 1)]
        def scores(j):
            cols = slice(LANES * j, LANES * (j + 1))
            q = q_ref[0, :, cols]
            q_lo, q_hi = jnp.where(lo, q, 0), jnp.where(lo, 0, q)
            qst = jnp.concatenate([part[GRID_W * g:GRID_W * (g + 1)] for g in range(ng) for part in (q_lo, q_hi)],
                                  axis=0)
            s_c = _dot_t(qst, k_ref[0, 0:CTX_LEN, cols])
            s_w = [_dot_t(grp(qst, g), k_ref[0, wins[g], cols])
                   + bias_ref[offs[g], 2 * j:2 * j + 2].reshape(sq, NA_WIN) for g in range(ng)]
            return s_c, s_w

        def numerators(s_c, s_w):
            pcs, pws, ls = [], [], []
            for g in range(ng):
                sc, sw = grp(s_c, g), s_w[g]
                m = jnp.maximum(jnp.max(sc, axis=-1, keepdims=True), jnp.max(sw, axis=-1, keepdims=True))
                pc, pw = jnp.exp2(sc - m), jnp.exp2(sw - m)
                ls.append(1.0 / (jnp.sum(pc, axis=-1, keepdims=True) + jnp.sum(pw, axis=-1, keepdims=True)))
                pcs.append(pc.astype(BF16))
                pws.append(pw.astype(BF16))
            return jnp.concatenate(pcs, axis=0), pws, ls

        def values(j, p_c, p_w, inv_l):
            cols = slice(LANES * j, LANES * (j + 1))
            o_c = _dot(p_c, v_ref[0, 0:CTX_LEN, cols])
            rows = [_unstack_heads((grp(o_c, g) + _dot(p_w[g], v_ref[0, wins[g], cols])) * inv_l[g])
                    for g in range(ng)]
            return jnp.concatenate(rows, axis=0)

        outs = []
        s_next = scores(0)
        for j in range(npair):
            s_cur = s_next
            if j + 1 < npair:
                s_next = scores(j + 1)
            outs.append(values(j, *numerators(*s_cur)))
        o_ref[0] = jnp.concatenate(outs, axis=-1).astype(BF16)

    if with_ctx:
        pl.when(t == 0)(ctx_block)
        pl.when(t > 0)(latent_rows)
    else:
        latent_rows()


def _na_attention(q, k, v, bias, with_ctx):
    nb, tt, _ = q.shape
    t_off = 0 if with_ctx else 1
    full = pl.BlockSpec((1, tt, NA_WIDTH), lambda b, i: (b, 0, 0))
    blk = pl.BlockSpec((1, TQ, NA_WIDTH), lambda b, i: (b, i + t_off, 0))
    return pl.pallas_call(
        functools.partial(_na_kernel, t_off=t_off, with_ctx=with_ctx),
        grid=(nb, tt // TQ - t_off),
        in_specs=[blk, full, full, _const_spec(bias.shape)],
        out_specs=blk,
        out_shape=jax.ShapeDtypeStruct((nb, tt, NA_WIDTH), BF16),
        compiler_params=pltpu.CompilerParams(vmem_limit_bytes=VMEM_LIMIT),
        name="na_attention",
    )(q, k, v, bias)


def _na_bias_table(rpb):
    qc = np.arange(GRID_W)[:, None]
    kc = np.arange(GRID_W)[None, :]
    cs = np.clip(qc - NA_KW // 2, 0, GRID_W - NA_KW)
    valid = (kc >= cs) & (kc < cs + NA_KW)
    dc = np.clip(kc - qc + NA_KW - 1, 0, 2 * NA_KW - 2)
    onehot = (dc[:, :, None] == np.arange(2 * NA_KW - 1)).astype(np.float32)
    per_row = jnp.einsum('hdc,qkc->hdqk', rpb, onehot, precision=lax.Precision.HIGHEST) * LOG2E
    per_row = jnp.where(valid, per_row, NEG)
    tab = jnp.stack([per_row[:, off:off + NA_KH] for off in range(NA_KH)])
    tab = jnp.transpose(tab, (0, 1, 3, 2, 4))
    return tab.reshape(NA_KH, NA_HEADS, GRID_W, NA_WIN).astype(F32)


MLA_TQ = 512


def _mla_kernel(q_ref, k_ref, v_ref, o_ref, *, with_ctx):
    i = pl.program_id(1)
    tt = k_ref.shape[1]

    def run(row0, nq, nk):
        rows = pl.ds(row0, nq)
        lo = lax.broadcasted_iota(jnp.int32, (nq, LANES), 1) < MLA_V

        def scores(hd):
            hs = slice(MLA_GROUP * hd, MLA_GROUP * (hd + 1))
            return _dot_t(q_ref[0, rows, hs], k_ref[0, 0:nk, hs])

        def values(hd, p, inv_l):
            vcols = slice(LANES * (hd // 2), LANES * (hd // 2 + 1))
            return _dot(p, v_ref[0, 0:nk, vcols]) * inv_l

        outs = []
        s_next = scores(0)
        for hd in range(MLA_HEADS):
            s = s_next
            if hd + 1 < MLA_HEADS:
                s_next = scores(hd + 1)
            p = jnp.exp2(s - jnp.max(s, axis=-1, keepdims=True))
            outs.append(values(hd, p.astype(BF16), 1.0 / jnp.sum(p, axis=-1, keepdims=True)))
        pairs = [jnp.where(lo, outs[2 * j], outs[2 * j + 1]) for j in range(MLA_WIDTH // LANES)]
        o_ref[0, rows, :] = jnp.concatenate(pairs, axis=-1).astype(BF16)

    latent_row0 = lambda blk: pl.multiple_of(CTX_LEN + blk * MLA_TQ, CTX_LEN)
    if with_ctx:
        pl.when(i == 0)(lambda: run(0, CTX_LEN, CTX_LEN))
        pl.when(i > 0)(lambda: run(latent_row0(i - 1), MLA_TQ, tt))
    else:
        run(latent_row0(i), MLA_TQ, tt)


def _mla_attention(q, k, v, with_ctx):
    nb, tt, _ = q.shape
    whole = lambda width: pl.BlockSpec((1, tt, width), lambda b, i: (b, 0, 0))
    return pl.pallas_call(
        functools.partial(_mla_kernel, with_ctx=with_ctx),
        grid=(nb, (tt - CTX_LEN) // MLA_TQ + (1 if with_ctx else 0)),
        in_specs=[whole(MLA_HEADS * MLA_GROUP), whole(MLA_HEADS * MLA_GROUP), whole(MLA_WIDTH)],
        out_specs=whole(MLA_WIDTH),
        out_shape=jax.ShapeDtypeStruct((nb, tt, MLA_WIDTH), BF16),
        compiler_params=pltpu.CompilerParams(vmem_limit_bytes=VMEM_LIMIT),
        name="mla_attention",
    )(q, k, v)


LRU_CHUNK = 256
LRU_TILES = LRU_CHUNK // SUBLANES


def _chunk_permutation():
    i = np.arange(LRU_CHUNK)
    p = np.zeros((LRU_CHUNK, LRU_CHUNK), np.float32)
    p[i, LRU_TILES * (i % SUBLANES) + i // SUBLANES] = 1.0
    return p


def _scan_tiles(a, b, carry, reverse):
    order = list(reversed(range(LRU_TILES))) if reverse else list(range(LRU_TILES))
    acum, bcum = [None] * LRU_TILES, [None] * LRU_TILES
    prev = None
    for k in order:
        if prev is None:
            acum[k], bcum[k] = a[k], b[k]
        else:
            acum[k] = a[k] * acum[prev]
            bcum[k] = a[k] * bcum[prev] + b[k]
        prev = k
    p_end, e_end = acum[prev], bcum[prev]
    sub = list(reversed(range(SUBLANES))) if reverse else list(range(SUBLANES))
    cin = [None] * SUBLANES
    for s in sub:
        cin[s] = carry
        carry = p_end[s:s + 1, :] * carry + e_end[s:s + 1, :]
    cin = jnp.concatenate(cin, axis=0)
    return [bcum[k] + acum[k] * cin for k in range(LRU_TILES)], carry


def _lru_kernel(lu_ref, lg_ref, perm_ref, permt_ref, cw_ref, cb_ref, wg_ref, bg_ref, lam_ref, o_ref,
                lp_ref, u_ref, hs_ref):
    tt = lu_ref.shape[1]
    nchunk = tt // LRU_CHUNK
    chunk_rows = lambda c: slice(c * LRU_CHUNK, (c + 1) * LRU_CHUNK)
    for c in range(nchunk):
        lp_ref[chunk_rows(c), :] = _dot(perm_ref[...], lu_ref[0, chunk_rows(c), :])

    left = CONV_W // 2
    sub = lax.broadcasted_iota(jnp.int32, (SUBLANES, LRU_W), 0)

    def conv_chunk(c, _):
        base = pl.multiple_of(c * LRU_CHUNK, LRU_CHUNK)
        tile = lambda b, k: lp_ref[pl.ds(b + SUBLANES * k, SUBLANES), :]
        prev_base = pl.multiple_of(jnp.maximum(c - 1, 0) * LRU_CHUNK, LRU_CHUNK)
        next_base = pl.multiple_of(jnp.minimum(c + 1, nchunk - 1) * LRU_CHUNK, LRU_CHUNK)
        has_prev = c >= 2
        has_next = (c >= 1) & (c < nchunk - 1)

        def earlier(k):
            edge = jnp.where(has_prev, tile(prev_base, k), 0.0)
            return pltpu.roll(jnp.where(sub == SUBLANES - 1, edge, tile(base, k)), 1, 0)

        def later(k):
            edge = jnp.where(has_next, tile(next_base, k), 0.0)
            return pltpu.roll(jnp.where(sub == 0, edge, tile(base, k)), SUBLANES - 1, 0)

        for k in range(LRU_TILES):
            acc = jnp.broadcast_to(cb_ref[...], (SUBLANES, LRU_W))
            for j in range(CONV_W):
                kk = k + j - left
                if kk < 0:
                    tap = earlier(kk + LRU_TILES)
                elif kk >= LRU_TILES:
                    tap = later(kk - LRU_TILES)
                else:
                    tap = tile(base, kk)
                acc = acc + tap * cw_ref[j:j + 1, :]
            u_ref[pl.ds(base + SUBLANES * k, SUBLANES), :] = acc
        return 0

    lax.fori_loop(0, nchunk, conv_chunk, 0)

    for d in (0, 1):
        reverse = d == 1
        nlam = -lam_ref[d]
        sp = jnp.maximum(nlam, 0.0) + jnp.log(1.0 + jnp.exp(-jnp.abs(nlam)))
        half_c = (0.5 * LRU_C) * sp

        def chunk_step(i, carry, d=d, reverse=reverse, half_c=half_c):
            c = jnp.where(i == 0, 0, nchunk - i) if reverse else i
            rows = pl.ds(pl.multiple_of(c * LRU_CHUNK, LRU_CHUNK), LRU_CHUNK)
            u = u_ref[rows, :]
            g = _dot(u.astype(BF16), wg_ref[d]) + bg_ref[d]
            ig = _sigmoid(g[:, LRU_W:])
            neg_log_a = half_c * jnp.tanh(0.5 * g[:, :LRU_W]) + half_c
            a = jnp.exp2(neg_log_a * (-LOG2E))
            bterm = jnp.sqrt(jnp.tanh(neg_log_a) * (a * a + 1.0)) * ig * u
            split = lambda z: [z[SUBLANES * k:SUBLANES * (k + 1), :] for k in range(LRU_TILES)]
            h, carry = _scan_tiles(split(a), split(bterm), carry, reverse)
            h = jnp.concatenate(h, axis=0)
            hs_ref[rows, :] = hs_ref[rows, :] + h if reverse else h
            return carry

        lax.fori_loop(0, nchunk, chunk_step, jnp.zeros((1, LRU_W), F32))

    k0 = float(np.sqrt(2.0 / np.pi))
    for c in range(nchunk):
        z = _dot(perm_ref[...], lg_ref[0, chunk_rows(c), :])
        gelu = z * (0.5 * (1.0 + jnp.tanh(k0 * (z + 0.044715 * (z * z * z)))))
        o_tile_order = (gelu * hs_ref[chunk_rows(c), :]).astype(BF16)
        o_ref[0, chunk_rows(c), :] = _dot(permt_ref[...], o_tile_order).astype(BF16)


def _lru(lu, lg, cw, cb, wg, bg, lam):
    nb, tt, _ = lu.shape
    full = pl.BlockSpec((1, tt, LRU_W), lambda b: (b, 0, 0))
    perm = _chunk_permutation()
    consts = (jnp.asarray(perm, BF16), jnp.asarray(perm.T, BF16), cw, cb, wg, bg, lam)
    return pl.pallas_call(
        _lru_kernel,
        grid=(nb,),
        in_specs=[full, full, *[_const_spec(a.shape) for a in consts]],
        out_specs=full,
        out_shape=jax.ShapeDtypeStruct((nb, tt, LRU_W), BF16),
        scratch_shapes=[pltpu.VMEM((tt, LRU_W), F32)] * 3,
        compiler_params=pltpu.CompilerParams(vmem_limit_bytes=VMEM_LIMIT),
        name="rglru",
    )(lu, lg, *consts)


FF_CHUNK = 2048


def _mix_mlp_kernel(x_ref, ona_ref, omla_ref, olru_ref, modb_ref, modc_ref, gmix_ref, gmlp_ref, wgt_ref, wna_ref,
                    wmla_ref, wlru_ref, wo_ref, w1_ref, w2_ref, o_ref, *, t_off):
    tm = x_ref.shape[1]
    is_ctx = (lax.broadcasted_iota(jnp.int32, (tm, 1), 0) < CTX_LEN) & (pl.program_id(1) + t_off == 0)
    mod = _mod_rows(modb_ref, modc_ref, is_ctx)
    x = x_ref[0]
    h = _modulated_norm(x, gmix_ref[...], mod(0), mod(1)).astype(BF16)
    gates = _sigmoid(_dot(h, wgt_ref[...]))
    y = None
    for g, (o_r, w_r) in enumerate(((ona_ref, wna_ref), (omla_ref, wmla_ref), (olru_ref, wlru_ref))):
        term = gates[:, D_MODEL * g:D_MODEL * (g + 1)] * _dot(o_r[0], w_r[...])
        y = term if y is None else y + term
    x1 = x + mod(2) * _dot(y.astype(BF16), wo_ref[...])

    h2 = _modulated_norm(x1, gmlp_ref[...], mod(3), mod(4)).astype(BF16)
    acc = jnp.zeros_like(x1)
    for c in range(D_FF // FF_CHUNK):
        a = jnp.maximum(_dot(h2, w1_ref[:, FF_CHUNK * c:FF_CHUNK * (c + 1)]), 0.0)
        acc = acc + _dot((a * a).astype(BF16), w2_ref[FF_CHUNK * c:FF_CHUNK * (c + 1), :])
    o_ref[0] = x1 + mod(5) * acc


def _mix_mlp(xs, ona, omla, olru, mod, gmix, gmlp, wgt, wna, wmla, wlru, wo, w1, w2, with_ctx):
    nb, tt, _ = xs.shape
    tm = TM_MIX if with_ctx else TQ
    t_off = 0 if with_ctx else CTX_LEN // tm
    nt = tt // tm - t_off
    tok = lambda width: pl.BlockSpec((1, tm, width), lambda b, i: (b, i + t_off, 0))
    return pl.pallas_call(
        functools.partial(_mix_mlp_kernel, t_off=t_off),
        grid=(nb, nt),
        in_specs=[
            tok(D_MODEL), tok(NA_WIDTH), tok(MLA_WIDTH), tok(LRU_W),
            pl.BlockSpec((1, 6, D_MODEL), lambda b, i: (b, 0, 0)),
            pl.BlockSpec((1, 6, D_MODEL), lambda b, i: (nb, 0, 0)),
            _const_spec(gmix.shape), _const_spec(gmlp.shape), _const_spec(wgt.shape), _const_spec(wna.shape),
            _const_spec(wmla.shape), _const_spec(wlru.shape), _const_spec(wo.shape), _const_spec(w1.shape),
            _const_spec(w2.shape),
        ],
        out_specs=pl.BlockSpec((1, tm, D_MODEL), lambda b, i: (b, i, 0)),
        out_shape=jax.ShapeDtypeStruct((nb, nt * tm, D_MODEL), F32),
        compiler_params=pltpu.CompilerParams(vmem_limit_bytes=VMEM_LIMIT),
        name="mix_mlp",
    )(xs, ona, omla, olru, mod, mod, gmix, gmlp, wgt, wna, wmla, wlru, wo, w1, w2)


def _group_source():
    nf = MLA_ROPE // 4
    half = LANES // 2
    src = np.full(LANES, -1)
    for part in (0, 1):
        lane0 = half * part
        src[lane0:lane0 + MLA_NOPE // 2] = MLA_NOPE // 2 * part + np.arange(MLA_NOPE // 2)
        src[lane0 + MLA_NOPE // 2:lane0 + MLA_NOPE // 2 + nf] = MLA_NOPE + nf * part + np.arange(nf)
        src[lane0 + MLA_NOPE // 2 + nf:lane0 + MLA_NOPE // 2 + 2 * nf] = MLA_NOPE + 2 * nf + nf * part + np.arange(nf)
    return src


_GROUP_SRC = _group_source()


def _to_group(v):
    return jnp.where(_GROUP_SRC >= 0, jnp.take(v, np.maximum(_GROUP_SRC, 0), axis=-1), 0)


def _heads_to_groups(w, width):
    k = w.shape[0]
    w = jnp.pad(w.reshape(k, MLA_HEADS, width), ((0, 0), (0, 0), (0, MLA_QK - width)))
    return _to_group(w).reshape(k, MLA_HEADS * MLA_GROUP)


def _relayout_w_in(w):
    o_mr = 3 * NA_WIDTH + MLA_Q_RANK + MLA_KV_RANK
    o_gt = o_mr + MLA_ROPE + 2 * LRU_W
    mr_group = _to_group(jnp.pad(w[:, o_mr:o_mr + MLA_ROPE], ((0, 0), (MLA_NOPE, 0))))
    w_proj = jnp.concatenate([w[:, :o_mr], mr_group, w[:, o_mr + MLA_ROPE:o_gt]], axis=1)
    return w_proj.astype(BF16), w[:, o_gt:].astype(BF16)


def _rope_tables(tt):
    p = np.arange(tt - CTX_LEN)
    nf = MLA_ROPE // 4
    inv = ROPE_BASE ** (-jnp.arange(nf, dtype=F32) / nf)
    cos, sin = [], []
    for pos in (jnp.asarray(p // GRID_W, F32), jnp.asarray(p % GRID_W, F32)):
        ang = pos[:, None] * inv
        cos += [jnp.cos(ang), jnp.cos(ang)]
        sin += [-jnp.sin(ang), jnp.sin(ang)]
    ones = jnp.ones((tt - CTX_LEN, MLA_NOPE), F32)
    rc = jnp.concatenate([jnp.ones((CTX_LEN, MLA_QK), F32), jnp.concatenate([ones] + cos, axis=1)], axis=0)
    rs = jnp.concatenate([jnp.zeros((CTX_LEN, MLA_QK), F32), jnp.concatenate([0.0 * ones] + sin, axis=1)], axis=0)
    return _to_group(rc), _to_group(rs)


def _block_diag(w):
    eye = jnp.eye(LRU_BLOCKS, dtype=w.dtype)
    return jnp.einsum('ncd,nm->ncmd', w, eye).reshape(LRU_W, LRU_W)


def kernel(x, c, ctx, c_ctx, w_mod, b_mod, g_mix, g_mlp, w_in, na_q_gain, na_k_gain, na_rpb, mla_qa_gain, w_q_b,
           mla_kva_gain, w_kv_b, mla_q_gain, mla_k_gain, lru_conv_w, lru_conv_b, lru_wa, lru_ba, lru_wx, lru_bx,
           lru_lambda, w_na_o, w_mla_o, w_lru_o, w_o, w_ff1, w_ff2):
    nb, seq, _ = x.shape
    depth = w_in.shape[0]
    assert ctx.shape[1] == CTX_LEN and seq // GRID_W == GRID_W // 2 and seq % GRID_W == 0
    tt = CTX_LEN + seq
    assert tt % TM_PROJ == 0 and tt % TM_MIX == 0 and tt % TQ == 0 and tt % LRU_CHUNK == 0

    xs = jnp.concatenate([ctx, x], axis=1)
    rp = -(-(nb + 1) // SUBLANES) * SUBLANES
    cvec = jnp.concatenate([c, c_ctx[None, :], jnp.zeros((rp - nb - 1, D_MODEL), F32)], axis=0)
    mod_all = _modulation(cvec, w_mod, b_mod).reshape(depth, rp, 6, D_MODEL)
    rc, rs = _rope_tables(tt)
    row = lambda v: v.reshape(1, -1)
    head_ind = jnp.asarray(np.kron(np.eye(NA_HEADS), np.ones((NA_DH, NA_DH))), BF16)

    for i in range(depth):
        with_ctx = i < depth - 1
        kvb = w_kv_b[i].reshape(MLA_KV_RANK, MLA_HEADS, MLA_NOPE + MLA_V)
        wk = _heads_to_groups(kvb[:, :, :MLA_NOPE].reshape(MLA_KV_RANK, -1), MLA_NOPE).astype(BF16)
        wv = kvb[:, :, MLA_NOPE:].reshape(MLA_KV_RANK, MLA_WIDTH).astype(BF16)
        w_proj, w_gate = _relayout_w_in(w_in[i])
        naq, nak, nav, qm, km, vm, lu, lg = _inproj(
            xs, mod_all[i], row(g_mix[i]), w_proj, head_ind,
            row(jnp.tile(na_q_gain[i], NA_HEADS) * (NA_DH ** -0.5 * LOG2E)), row(jnp.tile(na_k_gain[i], NA_HEADS)),
            row(mla_qa_gain[i]), _heads_to_groups(w_q_b[i], MLA_QK).astype(BF16), row(mla_kva_gain[i]), wk, wv,
            row(_to_group(mla_q_gain[i]) * (MLA_QK ** -0.5 * LOG2E)), row(_to_group(mla_k_gain[i])), rc, rs)

        o_na = _na_attention(naq, nak, nav, _na_bias_table(na_rpb[i]), with_ctx)
        o_mla = _mla_attention(qm, km, vm, with_ctx)

        wg = jnp.stack([jnp.concatenate([_block_diag(lru_wa[i, d]), _block_diag(lru_wx[i, d])], axis=1)
                        for d in (0, 1)]).astype(BF16)
        bg = jnp.concatenate([lru_ba[i], lru_bx[i]], axis=1).reshape(2, 1, 2 * LRU_W)
        o_lru = _lru(lu, lg, lru_conv_w[i], row(lru_conv_b[i]), wg, bg, lru_lambda[i].reshape(2, 1, LRU_W))

        xs = _mix_mlp(xs, o_na, o_mla, o_lru, mod_all[i], row(g_mix[i]), row(g_mlp[i]), w_gate,
                      w_na_o[i].astype(BF16), w_mla_o[i].astype(BF16), w_lru_o[i].astype(BF16),
                      w_o[i].astype(BF16), w_ff1[i].astype(BF16), w_ff2[i].astype(BF16), with_ctx)
    return xs
```

```python
import functools

import numpy as np
import jax
import jax.numpy as jnp
from jax import lax
from jax.experimental import pallas as pl
from jax.experimental.pallas import tpu as pltpu

F32 = jnp.float32
BF16 = jnp.bfloat16

D_MODEL = 1024
CTX_LEN = 256
GRID_W = 64
NA_HEADS = 8
NA_DH = 64
NA_KH = 8
NA_KW = 16
NA_WIDTH = NA_HEADS * NA_DH
MLA_HEADS = 8
MLA_NOPE = 64
MLA_ROPE = 32
MLA_V = 64
MLA_QK = MLA_NOPE + MLA_ROPE
MLA_Q_RANK = 384
MLA_KV_RANK = 256
MLA_WIDTH = MLA_HEADS * MLA_V
LRU_W = 512
LRU_BLOCKS = 8
LRU_BS = LRU_W // LRU_BLOCKS
LRU_C = 8.0
CONV_W = 4
D_FF = 4 * D_MODEL
ROPE_BASE = 10000.0
EPS = 1e-6
NEG = -1e30
LOG2E = float(np.log2(np.e))

LANES = 128
SUBLANES = 8
TQ = 256
TM_PROJ = 768
TM_MIX = 576
MLA_GROUP = LANES
VMEM_LIMIT = 56 * 1024 * 1024

C_NAQ, C_NAK, C_NAV = 0, 512, 1024
C_MQ = 1536
C_MKV = C_MQ + MLA_Q_RANK
C_MR = C_MKV + MLA_KV_RANK
C_LU = C_MR + LANES
C_LG = C_LU + LRU_W
C_END = C_LG + LRU_W


def _sigmoid(z):
    return 0.5 * jnp.tanh(0.5 * z) + 0.5


def _dot(a, b):
    return jnp.dot(a, b, preferred_element_type=F32)


def _dot_t(a, b):
    return lax.dot_general(a, b, (((1,), (1,)), ((), ())), preferred_element_type=F32)


def _const_spec(shape):
    nd = len(shape)
    return pl.BlockSpec(shape, lambda *_: (0,) * nd, pipeline_mode=pl.Buffered(1))


def _layer_spec(arr, layer):
    nd = arr.ndim
    return pl.BlockSpec((None,) + arr.shape[1:], lambda *_: (layer,) + (0,) * (nd - 1), pipeline_mode=pl.Buffered(1))


def _mod_specs(mod_all, layer, nb):
    shape = (None, 1) + mod_all.shape[2:]
    return [pl.BlockSpec(shape, lambda b, t: (layer, b, 0, 0)), pl.BlockSpec(shape, lambda b, t: (layer, nb, 0, 0))]


def _modulated_norm(x, gain, shift, scale):
    ms = jnp.mean(x * x, axis=-1, keepdims=True)
    return (x * lax.rsqrt(ms + EPS) * gain) * (1.0 + scale) + shift


def _mod_rows(modb_ref, modc_ref, is_ctx):
    return lambda k: jnp.where(is_ctx, modc_ref[0, k:k + 1, :], modb_ref[0, k:k + 1, :])


def _mod_kernel(c_ref, w_ref, b_ref, o_ref):
    cv = c_ref[...]
    s = cv * _sigmoid(cv)
    o_ref[0] = _dot(s.astype(BF16), w_ref[0].astype(BF16)) + b_ref[0]


def _modulation(cvec, w_mod, b_mod):
    depth = w_mod.shape[0]
    rp = cvec.shape[0]
    return pl.pallas_call(
        _mod_kernel,
        grid=(depth, 6),
        in_specs=[
            pl.BlockSpec((rp, D_MODEL), lambda i, j: (0, 0)),
            pl.BlockSpec((1, D_MODEL, D_MODEL), lambda i, j: (i, 0, j)),
            pl.BlockSpec((1, 1, D_MODEL), lambda i, j: (i, 0, j)),
        ],
        out_specs=pl.BlockSpec((1, rp, D_MODEL), lambda i, j: (i, 0, j)),
        out_shape=jax.ShapeDtypeStruct((depth, rp, 6 * D_MODEL), F32),
        name="modulation",
    )(cvec, w_mod, b_mod.reshape(depth, 1, 6 * D_MODEL))


def _head_norm64(p, ind, gain):
    ssq = _dot((p * p).astype(BF16), ind)
    return p * lax.rsqrt(ssq * (1.0 / NA_DH) + EPS) * gain


def _rope(n, rc, rs):
    return n * rc + pltpu.roll(n, LANES // 2, 1) * rs


def _inproj_kernel(x_ref, modb_ref, modc_ref, gmix_ref, w_ref, ind_ref, naqg_ref, nakg_ref, qag_ref, wqb_ref,
                   kvag_ref, wk_ref, wv_ref, mqg_ref, mkg_ref, rc_ref, rs_ref,
                   naq_ref, nak_ref, nav_ref, qm_ref, km_ref, vm_ref, lu_ref, lg_ref):
    tm = x_ref.shape[1]
    is_ctx = (lax.broadcasted_iota(jnp.int32, (tm, 1), 0) < CTX_LEN) & (pl.program_id(1) == 0)
    mod = _mod_rows(modb_ref, modc_ref, is_ctx)
    h = _modulated_norm(x_ref[0], gmix_ref[...], mod(0), mod(1)).astype(BF16)

    def proj(a, b):
        return _dot(h, w_ref[:, a:b])

    rc, rs = rc_ref[...], rs_ref[...]

    pq = proj(C_MQ, C_MKV)
    nq = pq * lax.rsqrt(jnp.mean(pq * pq, axis=-1, keepdims=True) + EPS) * qag_ref[...]
    q0 = _dot(nq.astype(BF16), wqb_ref[...])

    pkv = proj(C_MKV, C_MR)
    nkv = (pkv * lax.rsqrt(jnp.mean(pkv * pkv, axis=-1, keepdims=True) + EPS) * kvag_ref[...]).astype(BF16)
    vm_ref[0] = _dot(nkv, wv_ref[...]).astype(BF16)
    k0 = _dot(nkv, wk_ref[...])
    kr = proj(C_MR, C_LU)
    kr_ss = jnp.sum(kr * kr, axis=-1, keepdims=True)
    kr_rot = _rope(kr * mkg_ref[...], rc, rs)

    def q_head(hd):
        sl = slice(MLA_GROUP * hd, MLA_GROUP * (hd + 1))
        blk = q0[:, sl]
        ss = jnp.sum(blk * blk, axis=-1, keepdims=True) * (1.0 / MLA_QK)
        qm_ref[0, :, sl] = _rope(blk * lax.rsqrt(ss + EPS) * mqg_ref[...], rc, rs).astype(BF16)

    def k_head(hd):
        sl = slice(MLA_GROUP * hd, MLA_GROUP * (hd + 1))
        blk = k0[:, sl]
        ss = (jnp.sum(blk * blk, axis=-1, keepdims=True) + kr_ss) * (1.0 / MLA_QK)
        km_ref[0, :, sl] = ((blk * mkg_ref[...] + kr_rot) * lax.rsqrt(ss + EPS)).astype(BF16)

    def wide(i):
        if i == 0:
            naq_ref[0] = _head_norm64(proj(C_NAQ, C_NAK), ind_ref[...], naqg_ref[...]).astype(BF16)
        elif i == 1:
            nak_ref[0] = _head_norm64(proj(C_NAK, C_NAV), ind_ref[...], nakg_ref[...]).astype(BF16)
        elif i == 2:
            nav_ref[0] = proj(C_NAV, C_MQ).astype(BF16)
        elif i == 3:
            lu_ref[0] = proj(C_LU, C_LG).astype(BF16)
        else:
            lg_ref[0] = proj(C_LG, C_END).astype(BF16)

    for i in range(5):
        wide(i)
        for hd in range(2 * i, min(2 * i + 2, MLA_HEADS)):
            q_head(hd)
            k_head(hd)


def _inproj(layer, xs, mod_all, ind, rc, rs, params):
    nb, tt, _ = xs.shape
    tm = TM_PROJ
    tok = lambda width: pl.BlockSpec((1, tm, width), lambda b, t: (b, t, 0))
    rope = pl.BlockSpec((tm, LANES), lambda b, t: (t, 0))
    out_widths = (NA_WIDTH, NA_WIDTH, NA_WIDTH, MLA_HEADS * MLA_GROUP, MLA_HEADS * MLA_GROUP, MLA_WIDTH,
                  LRU_W, LRU_W)
    gmix, w, *rest = params
    return pl.pallas_call(
        _inproj_kernel,
        grid=(nb, tt // tm),
        in_specs=[
            tok(D_MODEL), *_mod_specs(mod_all, layer, nb),
            _layer_spec(gmix, layer), _layer_spec(w, layer), _const_spec(ind.shape),
            *[_layer_spec(a, layer) for a in rest],
            rope, rope,
        ],
        out_specs=[tok(wd) for wd in out_widths],
        out_shape=[jax.ShapeDtypeStruct((nb, tt, wd), BF16) for wd in out_widths],
        compiler_params=pltpu.CompilerParams(vmem_limit_bytes=VMEM_LIMIT),
        name="inproj",
    )(xs, mod_all, mod_all, gmix, w, ind, *rest, rc, rs)


NA_WIN = NA_KH * GRID_W
NA_TQ = 512
NA_ROWS_PER_STEP = NA_TQ // GRID_W


def _softmax_pv(scores, values):
    m = functools.reduce(jnp.maximum, [jnp.max(s, axis=-1, keepdims=True) for s in scores])
    ps = [jnp.exp2(s - m) for s in scores]
    l = functools.reduce(jnp.add, [jnp.sum(p, axis=-1, keepdims=True) for p in ps])
    o = functools.reduce(jnp.add, [_dot(p.astype(BF16), v) for p, v in zip(ps, values)])
    return o * (1.0 / l)


def _stack_heads(q):
    lo = lax.broadcasted_iota(jnp.int32, q.shape, 1) < NA_DH
    return jnp.concatenate([jnp.where(lo, q, 0), jnp.where(lo, 0, q)], axis=0)


def _unstack_heads(o):
    n = o.shape[0] // 2
    lo = lax.broadcasted_iota(jnp.int32, (n, LANES), 1) < NA_DH
    return jnp.where(lo, o[:n], o[n:])


def _na_kernel(q_ref, k_ref, v_ref, bias_ref, o_ref, *, with_ctx):
    i = pl.program_id(1)
    npair = NA_WIDTH // LANES

    def ctx_block():
        outs = []
        for j in range(npair):
            cols = slice(LANES * j, LANES * (j + 1))
            s_c = _dot_t(_stack_heads(q_ref[0, 0:CTX_LEN, cols]), k_ref[0, 0:CTX_LEN, cols])
            outs.append(_unstack_heads(_softmax_pv([s_c], [v_ref[0, 0:CTX_LEN, cols]])))
        o_ref[0, 0:CTX_LEN, :] = jnp.concatenate(outs, axis=-1).astype(BF16)

    def latent_rows(blk):
        ng = NA_ROWS_PER_STEP
        qrows = pl.ds(pl.multiple_of(CTX_LEN + blk * NA_TQ, CTX_LEN), NA_TQ)
        sq = 2 * GRID_W
        offs, wins = [], []
        for g in range(ng):
            r = blk * ng + g
            rs = jnp.clip(r - NA_KH // 2, 0, GRID_W // 2 - NA_KH)
            offs.append(rs - r + NA_KH - 1)
            wins.append(pl.ds(pl.multiple_of(CTX_LEN + rs * GRID_W, GRID_W), NA_WIN))
        lo = lax.broadcasted_iota(jnp.int32, (NA_TQ, LANES), 1) < NA_DH
        grp = lambda z, g: z[sq * g:sq * (g + 1)]

        def scores(j):
            cols = slice(LANES * j, LANES * (j + 1))
            q = q_ref[0, qrows, cols]
            q_lo, q_hi = jnp.where(lo, q, 0), jnp.where(lo, 0, q)
            qst = jnp.concatenate([part[GRID_W * g:GRID_W * (g + 1)] for g in range(ng) for part in (q_lo, q_hi)],
                                  axis=0)
            s_c = _dot_t(qst, k_ref[0, 0:CTX_LEN, cols])
            s_w = [_dot_t(grp(qst, g), k_ref[0, wins[g], cols])
                   + bias_ref[offs[g], 2 * j:2 * j + 2].reshape(sq, NA_WIN) for g in range(ng)]
            return s_c, s_w

        def numerators(s_c, s_w):
            pcs, pws, ls = [], [], []
            for g in range(ng):
                sc, sw = grp(s_c, g), s_w[g]
                m = jnp.maximum(jnp.max(sc, axis=-1, keepdims=True), jnp.max(sw, axis=-1, keepdims=True))
                pc, pw = jnp.exp2(sc - m), jnp.exp2(sw - m)
                ls.append(1.0 / (jnp.sum(pc, axis=-1, keepdims=True) + jnp.sum(pw, axis=-1, keepdims=True)))
                pcs.append(pc.astype(BF16))
                pws.append(pw.astype(BF16))
            return jnp.concatenate(pcs, axis=0), pws, ls

        def values(j, p_c, p_w, inv_l):
            cols = slice(LANES * j, LANES * (j + 1))
            o_c = _dot(p_c, v_ref[0, 0:CTX_LEN, cols])
            rows = [_unstack_heads((grp(o_c, g) + _dot(p_w[g], v_ref[0, wins[g], cols])) * inv_l[g])
                    for g in range(ng)]
            return jnp.concatenate(rows, axis=0)

        outs = []
        s_next = scores(0)
        for j in range(npair):
            s_cur = s_next
            if j + 1 < npair:
                s_next = scores(j + 1)
            outs.append(values(j, *numerators(*s_cur)))
        o_ref[0, qrows, :] = jnp.concatenate(outs, axis=-1).astype(BF16)

    if with_ctx:
        pl.when(i == 0)(ctx_block)
        pl.when(i > 0)(lambda: latent_rows(i - 1))
    else:
        latent_rows(i)


def _na_attention(layer, q, k, v, bias, with_ctx):
    nb, tt, _ = q.shape
    full = pl.BlockSpec((1, tt, NA_WIDTH), lambda b, i: (b, 0, 0))
    return pl.pallas_call(
        functools.partial(_na_kernel, with_ctx=with_ctx),
        grid=(nb, (tt - CTX_LEN) // NA_TQ + (1 if with_ctx else 0)),
        in_specs=[full, full, full, _layer_spec(bias, layer)],
        out_specs=full,
        out_shape=jax.ShapeDtypeStruct((nb, tt, NA_WIDTH), BF16),
        compiler_params=pltpu.CompilerParams(vmem_limit_bytes=VMEM_LIMIT),
        name="na_attention",
    )(q, k, v, bias)


def _na_bias_table(rpb):
    qc = np.arange(GRID_W)[:, None]
    kc = np.arange(GRID_W)[None, :]
    cs = np.clip(qc - NA_KW // 2, 0, GRID_W - NA_KW)
    valid = (kc >= cs) & (kc < cs + NA_KW)
    dc = np.clip(kc - qc + NA_KW - 1, 0, 2 * NA_KW - 2)
    onehot = (dc[:, :, None] == np.arange(2 * NA_KW - 1)).astype(np.float32)
    per_row = jnp.einsum('lhdc,qkc->lhdqk', rpb, onehot, precision=lax.Precision.HIGHEST) * LOG2E
    per_row = jnp.where(valid, per_row, NEG)
    tab = jnp.stack([per_row[:, :, off:off + NA_KH] for off in range(NA_KH)], axis=1)
    tab = jnp.transpose(tab, (0, 1, 2, 4, 3, 5))
    return tab.reshape(rpb.shape[0], NA_KH, NA_HEADS, GRID_W, NA_WIN).astype(F32)


MLA_TQ = 512


def _mla_kernel(q_ref, k_ref, v_ref, o_ref, *, with_ctx):
    i = pl.program_id(1)
    tt = k_ref.shape[1]

    def run(row0, nq, nk):
        rows = pl.ds(row0, nq)
        lo = lax.broadcasted_iota(jnp.int32, (nq, LANES), 1) < MLA_V

        def scores(hd):
            hs = slice(MLA_GROUP * hd, MLA_GROUP * (hd + 1))
            return _dot_t(q_ref[0, rows, hs], k_ref[0, 0:nk, hs])

        def values(hd, p, inv_l):
            vcols = slice(LANES * (hd // 2), LANES * (hd // 2 + 1))
            return _dot(p, v_ref[0, 0:nk, vcols]) * inv_l

        outs = []
        s_next = scores(0)
        for hd in range(MLA_HEADS):
            s = s_next
            if hd + 1 < MLA_HEADS:
                s_next = scores(hd + 1)
            p = jnp.exp2(s - jnp.max(s, axis=-1, keepdims=True))
            outs.append(values(hd, p.astype(BF16), 1.0 / jnp.sum(p, axis=-1, keepdims=True)))
        pairs = [jnp.where(lo, outs[2 * j], outs[2 * j + 1]) for j in range(MLA_WIDTH // LANES)]
        o_ref[0, rows, :] = jnp.concatenate(pairs, axis=-1).astype(BF16)

    latent_row0 = lambda blk: pl.multiple_of(CTX_LEN + blk * MLA_TQ, CTX_LEN)
    if with_ctx:
        pl.when(i == 0)(lambda: run(0, CTX_LEN, CTX_LEN))
        pl.when(i > 0)(lambda: run(latent_row0(i - 1), MLA_TQ, tt))
    else:
        run(latent_row0(i), MLA_TQ, tt)


def _mla_attention(q, k, v, with_ctx):
    nb, tt, _ = q.shape
    whole = lambda width: pl.BlockSpec((1, tt, width), lambda b, i: (b, 0, 0))
    return pl.pallas_call(
        functools.partial(_mla_kernel, with_ctx=with_ctx),
        grid=(nb, (tt - CTX_LEN) // MLA_TQ + (1 if with_ctx else 0)),
        in_specs=[whole(MLA_HEADS * MLA_GROUP), whole(MLA_HEADS * MLA_GROUP), whole(MLA_WIDTH)],
        out_specs=whole(MLA_WIDTH),
        out_shape=jax.ShapeDtypeStruct((nb, tt, MLA_WIDTH), BF16),
        compiler_params=pltpu.CompilerParams(vmem_limit_bytes=VMEM_LIMIT),
        name="mla_attention",
    )(q, k, v)


LRU_CHUNK = 256
LRU_TILES = LRU_CHUNK // SUBLANES


def _chunk_permutation():
    i = np.arange(LRU_CHUNK)
    p = np.zeros((LRU_CHUNK, LRU_CHUNK), np.float32)
    p[i, LRU_TILES * (i % SUBLANES) + i // SUBLANES] = 1.0
    return p


def _scan_tiles(a, b, carry, reverse):
    order = list(reversed(range(LRU_TILES))) if reverse else list(range(LRU_TILES))
    acum, bcum = [None] * LRU_TILES, [None] * LRU_TILES
    prev = None
    for k in order:
        if prev is None:
            acum[k], bcum[k] = a[k], b[k]
        else:
            acum[k] = a[k] * acum[prev]
            bcum[k] = a[k] * bcum[prev] + b[k]
        prev = k
    p_end, e_end = acum[prev], bcum[prev]
    sub = list(reversed(range(SUBLANES))) if reverse else list(range(SUBLANES))
    cin = [None] * SUBLANES
    for s in sub:
        cin[s] = carry
        carry = p_end[s:s + 1, :] * carry + e_end[s:s + 1, :]
    cin = jnp.concatenate(cin, axis=0)
    return [bcum[k] + acum[k] * cin for k in range(LRU_TILES)], carry


def _lru_kernel(lu_ref, lg_ref, perm_ref, permt_ref, cw_ref, cb_ref, wg_ref, bg_ref, lam_ref, o_ref,
                lp_ref, u_ref, hs_ref):
    tt = lu_ref.shape[1]
    nchunk = tt // LRU_CHUNK
    chunk_rows = lambda c: slice(c * LRU_CHUNK, (c + 1) * LRU_CHUNK)
    for c in range(nchunk):
        lp_ref[chunk_rows(c), :] = _dot(perm_ref[...], lu_ref[0, chunk_rows(c), :])

    left = CONV_W // 2
    sub = lax.broadcasted_iota(jnp.int32, (SUBLANES, LRU_W), 0)

    def conv_chunk(c, _):
        base = pl.multiple_of(c * LRU_CHUNK, LRU_CHUNK)
        tile = lambda b, k: lp_ref[pl.ds(b + SUBLANES * k, SUBLANES), :]
        prev_base = pl.multiple_of(jnp.maximum(c - 1, 0) * LRU_CHUNK, LRU_CHUNK)
        next_base = pl.multiple_of(jnp.minimum(c + 1, nchunk - 1) * LRU_CHUNK, LRU_CHUNK)
        has_prev = c >= 2
        has_next = (c >= 1) & (c < nchunk - 1)

        def earlier(k):
            edge = jnp.where(has_prev, tile(prev_base, k), 0.0)
            return pltpu.roll(jnp.where(sub == SUBLANES - 1, edge, tile(base, k)), 1, 0)

        def later(k):
            edge = jnp.where(has_next, tile(next_base, k), 0.0)
            return pltpu.roll(jnp.where(sub == 0, edge, tile(base, k)), SUBLANES - 1, 0)

        for k in range(LRU_TILES):
            acc = jnp.broadcast_to(cb_ref[...], (SUBLANES, LRU_W))
            for j in range(CONV_W):
                kk = k + j - left
                if kk < 0:
                    tap = earlier(kk + LRU_TILES)
                elif kk >= LRU_TILES:
                    tap = later(kk - LRU_TILES)
                else:
                    tap = tile(base, kk)
                acc = acc + tap * cw_ref[j:j + 1, :]
            u_ref[pl.ds(base + SUBLANES * k, SUBLANES), :] = acc
        return 0

    lax.fori_loop(0, nchunk, conv_chunk, 0)

    for d in (0, 1):
        reverse = d == 1
        nlam = -lam_ref[d]
        sp = jnp.maximum(nlam, 0.0) + jnp.log(1.0 + jnp.exp(-jnp.abs(nlam)))
        half_c = (0.5 * LRU_C) * sp

        def chunk_step(i, carry, d=d, reverse=reverse, half_c=half_c):
            c = jnp.where(i == 0, 0, nchunk - i) if reverse else i
            rows = pl.ds(pl.multiple_of(c * LRU_CHUNK, LRU_CHUNK), LRU_CHUNK)
            u = u_ref[rows, :]
            g = _dot(u.astype(BF16), wg_ref[d]) + bg_ref[d]
            ig = _sigmoid(g[:, LRU_W:])
            neg_log_a = half_c * jnp.tanh(0.5 * g[:, :LRU_W]) + half_c
            a = jnp.exp2(neg_log_a * (-LOG2E))
            bterm = jnp.sqrt(jnp.tanh(neg_log_a) * (a * a + 1.0)) * ig * u
            split = lambda z: [z[SUBLANES * k:SUBLANES * (k + 1), :] for k in range(LRU_TILES)]
            h, carry = _scan_tiles(split(a), split(bterm), carry, reverse)
            h = jnp.concatenate(h, axis=0)
            hs_ref[rows, :] = hs_ref[rows, :] + h if reverse else h
            return carry

        lax.fori_loop(0, nchunk, chunk_step, jnp.zeros((1, LRU_W), F32))

    k0 = float(np.sqrt(2.0 / np.pi))
    for c in range(nchunk):
        z = _dot(perm_ref[...], lg_ref[0, chunk_rows(c), :])
        gelu = z * (0.5 * (1.0 + jnp.tanh(k0 * (z + 0.044715 * (z * z * z)))))
        o_tile_order = (gelu * hs_ref[chunk_rows(c), :]).astype(BF16)
        o_ref[0, chunk_rows(c), :] = _dot(permt_ref[...], o_tile_order).astype(BF16)


def _lru(layer, lu, lg, params):
    nb, tt, _ = lu.shape
    full = pl.BlockSpec((1, tt, LRU_W), lambda b: (b, 0, 0))
    perm = _chunk_permutation()
    perms = (jnp.asarray(perm, BF16), jnp.asarray(perm.T, BF16))
    consts = perms + tuple(params)
    return pl.pallas_call(
        _lru_kernel,
        grid=(nb,),
        in_specs=[full, full, *[_const_spec(a.shape) for a in perms], *[_layer_spec(a, layer) for a in params]],
        out_specs=full,
        out_shape=jax.ShapeDtypeStruct((nb, tt, LRU_W), BF16),
        scratch_shapes=[pltpu.VMEM((tt, LRU_W), F32)] * 3,
        compiler_params=pltpu.CompilerParams(vmem_limit_bytes=VMEM_LIMIT),
        name="rglru",
    )(lu, lg, *consts)


FF_CHUNK = 2048


def _mix_mlp_kernel(x_ref, ona_ref, omla_ref, olru_ref, modb_ref, modc_ref, gmix_ref, gmlp_ref, wgt_ref, wna_ref,
                    wmla_ref, wlru_ref, wo_ref, w1_ref, w2_ref, o_ref, *, t_off):
    tm = x_ref.shape[1]
    is_ctx = (lax.broadcasted_iota(jnp.int32, (tm, 1), 0) < CTX_LEN) & (pl.program_id(1) + t_off == 0)
    mod = _mod_rows(modb_ref, modc_ref, is_ctx)
    x = x_ref[0]
    h = _modulated_norm(x, gmix_ref[...], mod(0), mod(1)).astype(BF16)
    gates = _sigmoid(_dot(h, wgt_ref[...]))
    y = None
    for g, (o_r, w_r) in enumerate(((ona_ref, wna_ref), (omla_ref, wmla_ref), (olru_ref, wlru_ref))):
        term = gates[:, D_MODEL * g:D_MODEL * (g + 1)] * _dot(o_r[0], w_r[...])
        y = term if y is None else y + term
    x1 = x + mod(2) * _dot(y.astype(BF16), wo_ref[...])

    h2 = _modulated_norm(x1, gmlp_ref[...], mod(3), mod(4)).astype(BF16)
    acc = jnp.zeros_like(x1)
    for c in range(D_FF // FF_CHUNK):
        a = jnp.maximum(_dot(h2, w1_ref[:, FF_CHUNK * c:FF_CHUNK * (c + 1)]), 0.0)
        acc = acc + _dot((a * a).astype(BF16), w2_ref[FF_CHUNK * c:FF_CHUNK * (c + 1), :])
    o_ref[0] = x1 + mod(5) * acc


def _mix_mlp(layer, xs, ona, omla, olru, mod_all, params, with_ctx):
    nb, tt, _ = xs.shape
    tm = TM_MIX if with_ctx else TQ
    t_off = 0 if with_ctx else CTX_LEN // tm
    nt = tt // tm - t_off
    tok = lambda width: pl.BlockSpec((1, tm, width), lambda b, i: (b, i + t_off, 0))
    return pl.pallas_call(
        functools.partial(_mix_mlp_kernel, t_off=t_off),
        grid=(nb, nt),
        in_specs=[
            tok(D_MODEL), tok(NA_WIDTH), tok(MLA_WIDTH), tok(LRU_W), *_mod_specs(mod_all, layer, nb),
            *[_layer_spec(a, layer) for a in params],
        ],
        out_specs=pl.BlockSpec((1, tm, D_MODEL), lambda b, i: (b, i, 0)),
        out_shape=jax.ShapeDtypeStruct((nb, nt * tm, D_MODEL), F32),
        compiler_params=pltpu.CompilerParams(vmem_limit_bytes=VMEM_LIMIT),
        name="mix_mlp",
    )(xs, ona, omla, olru, mod_all, mod_all, *params)


def _group_source():
    nf = MLA_ROPE // 4
    half = LANES // 2
    src = np.full(LANES, -1)
    for part in (0, 1):
        lane0 = half * part
        src[lane0:lane0 + MLA_NOPE // 2] = MLA_NOPE // 2 * part + np.arange(MLA_NOPE // 2)
        src[lane0 + MLA_NOPE // 2:lane0 + MLA_NOPE // 2 + nf] = MLA_NOPE + nf * part + np.arange(nf)
        src[lane0 + MLA_NOPE // 2 + nf:lane0 + MLA_NOPE // 2 + 2 * nf] = MLA_NOPE + 2 * nf + nf * part + np.arange(nf)
    return src


_GROUP_SRC = _group_source()


def _to_group(v):
    return jnp.where(_GROUP_SRC >= 0, jnp.take(v, np.maximum(_GROUP_SRC, 0), axis=-1), 0)


def _pad_last(v, before, after):
    return jnp.pad(v, [(0, 0)] * (v.ndim - 1) + [(before, after)])


def _heads_to_groups(w, width):
    return _to_group(_pad_last(w, 0, MLA_QK - width)).reshape(*w.shape[:-2], MLA_HEADS * MLA_GROUP)


def _relayout_w_in(w):
    o_mr = 3 * NA_WIDTH + MLA_Q_RANK + MLA_KV_RANK
    o_gt = o_mr + MLA_ROPE + 2 * LRU_W
    mr_group = _to_group(_pad_last(w[..., o_mr:o_mr + MLA_ROPE], MLA_NOPE, 0))
    w_proj = jnp.concatenate([w[..., :o_mr], mr_group, w[..., o_mr + MLA_ROPE:o_gt]], axis=-1)
    return w_proj.astype(BF16), w[..., o_gt:].astype(BF16)


def _rope_tables(tt):
    p = np.arange(tt - CTX_LEN)
    nf = MLA_ROPE // 4
    inv = ROPE_BASE ** (-jnp.arange(nf, dtype=F32) / nf)
    cos, sin = [], []
    for pos in (jnp.asarray(p // GRID_W, F32), jnp.asarray(p % GRID_W, F32)):
        ang = pos[:, None] * inv
        cos += [jnp.cos(ang), jnp.cos(ang)]
        sin += [-jnp.sin(ang), jnp.sin(ang)]
    ones = jnp.ones((tt - CTX_LEN, MLA_NOPE), F32)
    rc = jnp.concatenate([jnp.ones((CTX_LEN, MLA_QK), F32), jnp.concatenate([ones] + cos, axis=1)], axis=0)
    rs = jnp.concatenate([jnp.zeros((CTX_LEN, MLA_QK), F32), jnp.concatenate([0.0 * ones] + sin, axis=1)], axis=0)
    return _to_group(rc), _to_group(rs)


def _block_diag(w):
    eye = jnp.eye(LRU_BLOCKS, dtype=w.dtype)
    return jnp.einsum('...ncd,nm->...ncmd', w, eye).reshape(*w.shape[:-3], LRU_W, LRU_W)


def kernel(x, c, ctx, c_ctx, w_mod, b_mod, g_mix, g_mlp, w_in, na_q_gain, na_k_gain, na_rpb, mla_qa_gain, w_q_b,
           mla_kva_gain, w_kv_b, mla_q_gain, mla_k_gain, lru_conv_w, lru_conv_b, lru_wa, lru_ba, lru_wx, lru_bx,
           lru_lambda, w_na_o, w_mla_o, w_lru_o, w_o, w_ff1, w_ff2):
    nb, seq, _ = x.shape
    depth = w_in.shape[0]
    assert ctx.shape[1] == CTX_LEN and seq // GRID_W == GRID_W // 2 and seq % GRID_W == 0
    tt = CTX_LEN + seq
    assert tt % TM_PROJ == 0 and tt % TM_MIX == 0 and tt % TQ == 0 and tt % LRU_CHUNK == 0

    xs = jnp.concatenate([ctx, x], axis=1)
    rp = -(-(nb + 1) // SUBLANES) * SUBLANES
    cvec = jnp.concatenate([c, c_ctx[None, :], jnp.zeros((rp - nb - 1, D_MODEL), F32)], axis=0)
    mod_all = _modulation(cvec, w_mod, b_mod).reshape(depth, rp, 6, D_MODEL)
    rc, rs = _rope_tables(tt)
    head_ind = jnp.asarray(np.kron(np.eye(NA_HEADS), np.ones((NA_DH, NA_DH))), BF16)

    rows = lambda v: v[:, None, :]
    bf = lambda v: v.astype(BF16)
    kvb = w_kv_b.reshape(depth, MLA_KV_RANK, MLA_HEADS, MLA_NOPE + MLA_V)
    w_proj, w_gate = _relayout_w_in(w_in)
    proj_params = (
        rows(g_mix), w_proj,
        rows(jnp.tile(na_q_gain, (1, NA_HEADS)) * (NA_DH ** -0.5 * LOG2E)), rows(jnp.tile(na_k_gain, (1, NA_HEADS))),
        rows(mla_qa_gain), bf(_heads_to_groups(w_q_b.reshape(depth, MLA_Q_RANK, MLA_HEADS, MLA_QK), MLA_QK)),
        rows(mla_kva_gain), bf(_heads_to_groups(kvb[..., :MLA_NOPE], MLA_NOPE)),
        bf(kvb[..., MLA_NOPE:].reshape(depth, MLA_KV_RANK, MLA_WIDTH)),
        rows(_to_group(mla_q_gain) * (MLA_QK ** -0.5 * LOG2E)), rows(_to_group(mla_k_gain)))
    na_bias = _na_bias_table(na_rpb)
    lru_params = (
        lru_conv_w, rows(lru_conv_b),
        bf(jnp.concatenate([_block_diag(lru_wa), _block_diag(lru_wx)], axis=-1)),
        jnp.concatenate([lru_ba, lru_bx], axis=-1)[:, :, None, :], lru_lambda[:, :, None, :])
    mix_params = (rows(g_mix), rows(g_mlp), w_gate, bf(w_na_o), bf(w_mla_o), bf(w_lru_o), bf(w_o), bf(w_ff1),
                  bf(w_ff2))

    for i in range(depth):
        with_ctx = i < depth - 1
        naq, nak, nav, qm, km, vm, lu, lg = _inproj(i, xs, mod_all, head_ind, rc, rs, proj_params)
        o_na = _na_attention(i, naq, nak, nav, na_bias, with_ctx)
        o_mla = _mla_attention(qm, km, vm, with_ctx)
        o_lru = _lru(i, lu, lg, lru_params)
        xs = _mix_mlp(i, xs, o_na, o_mla, o_lru, mod_all, mix_params, with_ctx)
    return xs
```

```python
import functools

import numpy as np
import jax
import jax.numpy as jnp
from jax import lax
from jax.experimental import pallas as pl
from jax.experimental.pallas import tpu as pltpu

F32 = jnp.float32
BF16 = jnp.bfloat16

D_MODEL = 1024
CTX_LEN = 256
GRID_W = 64
NA_HEADS = 8
NA_DH = 64
NA_KH = 8
NA_KW = 16
NA_WIDTH = NA_HEADS * NA_DH
MLA_HEADS = 8
MLA_NOPE = 64
MLA_ROPE = 32
MLA_V = 64
MLA_QK = MLA_NOPE + MLA_ROPE
MLA_Q_RANK = 384
MLA_KV_RANK = 256
MLA_WIDTH = MLA_HEADS * MLA_V
LRU_W = 512
LRU_BLOCKS = 8
LRU_BS = LRU_W // LRU_BLOCKS
LRU_C = 8.0
CONV_W = 4
D_FF = 4 * D_MODEL
ROPE_BASE = 10000.0
EPS = 1e-6
NEG = -1e30
LOG2E = float(np.log2(np.e))
F32_TINY = float(np.finfo(np.float32).tiny)

LANES = 128
SUBLANES = 8
TQ = 256
TM_PROJ = 768
TM_MIX = 576
MLA_GROUP = LANES
VMEM_LIMIT = 56 * 1024 * 1024

C_NAQ, C_NAK, C_NAV = 0, 512, 1024
C_MQ = 1536
C_MKV = C_MQ + MLA_Q_RANK
C_MR = C_MKV + MLA_KV_RANK
C_LU = C_MR + LANES
C_LG = C_LU + LRU_W
C_END = C_LG + LRU_W


def _sigmoid(z):
    return 0.5 * jnp.tanh(0.5 * z) + 0.5


def _dot(a, b):
    return jnp.dot(a, b, preferred_element_type=F32)


def _dot_t(a, b):
    return lax.dot_general(a, b, (((1,), (1,)), ((), ())), preferred_element_type=F32)


def _const_spec(shape):
    nd = len(shape)
    return pl.BlockSpec(shape, lambda *_: (0,) * nd, pipeline_mode=pl.Buffered(1))


def _layer_spec(arr, layer):
    nd = arr.ndim
    return pl.BlockSpec((None,) + arr.shape[1:], lambda *_: (layer,) + (0,) * (nd - 1), pipeline_mode=pl.Buffered(1))


def _mod_specs(mod_all, layer, nb):
    shape = (None, 1) + mod_all.shape[2:]
    return [pl.BlockSpec(shape, lambda b, t: (layer, b, 0, 0)), pl.BlockSpec(shape, lambda b, t: (layer, nb, 0, 0))]


def _modulated_norm(x, gain, shift, scale):
    ms = jnp.mean(x * x, axis=-1, keepdims=True)
    return (x * lax.rsqrt(ms + EPS) * gain) * (1.0 + scale) + shift


def _mod_rows(modb_ref, modc_ref, is_ctx):
    return lambda k: jnp.where(is_ctx, modc_ref[0, k:k + 1, :], modb_ref[0, k:k + 1, :])


def _mod_kernel(c_ref, w_ref, b_ref, o_ref):
    cv = c_ref[...]
    s = cv * _sigmoid(cv)
    o_ref[0] = _dot(s.astype(BF16), w_ref[0].astype(BF16)) + b_ref[0]


def _modulation(cvec, w_mod, b_mod):
    depth = w_mod.shape[0]
    rp = cvec.shape[0]
    return pl.pallas_call(
        _mod_kernel,
        grid=(depth, 6),
        in_specs=[
            pl.BlockSpec((rp, D_MODEL), lambda i, j: (0, 0)),
            pl.BlockSpec((1, D_MODEL, D_MODEL), lambda i, j: (i, 0, j)),
            pl.BlockSpec((1, 1, D_MODEL), lambda i, j: (i, 0, j)),
        ],
        out_specs=pl.BlockSpec((1, rp, D_MODEL), lambda i, j: (i, 0, j)),
        out_shape=jax.ShapeDtypeStruct((depth, rp, 6 * D_MODEL), F32),
        name="modulation",
    )(cvec, w_mod, b_mod.reshape(depth, 1, 6 * D_MODEL))


def _head_norm64(p, ind, gain):
    ssq = _dot((p * p).astype(BF16), ind)
    return p * lax.rsqrt(ssq * (1.0 / NA_DH) + EPS) * gain


def _rope(n, rc, rs):
    return n * rc + pltpu.roll(n, LANES // 2, 1) * rs


def _inproj_kernel(x_ref, modb_ref, modc_ref, gmix_ref, w_ref, ind_ref, naqg_ref, nakg_ref, qag_ref, wqb_ref,
                   kvag_ref, wkv_ref, mqg_ref, mkg_ref, rc_ref, rs_ref,
                   naq_ref, nak_ref, nav_ref, qm_ref, km_ref, vm_ref, lu_ref, lg_ref):
    tm = x_ref.shape[1]
    is_ctx = (lax.broadcasted_iota(jnp.int32, (tm, 1), 0) < CTX_LEN) & (pl.program_id(1) == 0)
    mod = _mod_rows(modb_ref, modc_ref, is_ctx)
    h = _modulated_norm(x_ref[0], gmix_ref[...], mod(0), mod(1)).astype(BF16)

    def proj(a, b):
        return _dot(h, w_ref[:, a:b])

    rc, rs = rc_ref[...], rs_ref[...]

    platent = proj(C_MQ, C_LU)
    pq = platent[:, :MLA_Q_RANK]
    pkv = platent[:, MLA_Q_RANK:MLA_Q_RANK + MLA_KV_RANK]
    kr = platent[:, MLA_Q_RANK + MLA_KV_RANK:]

    nq = pq * lax.rsqrt(jnp.mean(pq * pq, axis=-1, keepdims=True) + EPS) * qag_ref[...]
    q0 = _dot(nq.astype(BF16), wqb_ref[...])

    nkv = (pkv * lax.rsqrt(jnp.mean(pkv * pkv, axis=-1, keepdims=True) + EPS) * kvag_ref[...]).astype(BF16)
    kv = _dot(nkv, wkv_ref[...])
    k0 = kv[:, :MLA_HEADS * MLA_GROUP]
    vm_ref[0] = kv[:, MLA_HEADS * MLA_GROUP:].astype(BF16)
    kr_ss = jnp.sum(kr * kr, axis=-1, keepdims=True)
    kr_rot = _rope(kr * mkg_ref[...], rc, rs)

    def q_head(hd):
        sl = slice(MLA_GROUP * hd, MLA_GROUP * (hd + 1))
        blk = q0[:, sl]
        ss = jnp.sum(blk * blk, axis=-1, keepdims=True) * (1.0 / MLA_QK)
        qm_ref[0, :, sl] = _rope(blk * lax.rsqrt(ss + EPS) * mqg_ref[...], rc, rs).astype(BF16)

    def k_head(hd):
        sl = slice(MLA_GROUP * hd, MLA_GROUP * (hd + 1))
        blk = k0[:, sl]
        ss = (jnp.sum(blk * blk, axis=-1, keepdims=True) + kr_ss) * (1.0 / MLA_QK)
        km_ref[0, :, sl] = ((blk * mkg_ref[...] + kr_rot) * lax.rsqrt(ss + EPS)).astype(BF16)

    def wide(i):
        if i == 0:
            naq_ref[0] = _head_norm64(proj(C_NAQ, C_NAK), ind_ref[...], naqg_ref[...]).astype(BF16)
        elif i == 1:
            nak_ref[0] = _head_norm64(proj(C_NAK, C_NAV), ind_ref[...], nakg_ref[...]).astype(BF16)
        elif i == 2:
            nav_ref[0] = proj(C_NAV, C_MQ).astype(BF16)
        elif i == 3:
            lu_ref[0] = proj(C_LU, C_LG).astype(BF16)
        else:
            lg_ref[0] = proj(C_LG, C_END).astype(BF16)

    for i in range(5):
        wide(i)
        for hd in range(2 * i, min(2 * i + 2, MLA_HEADS)):
            q_head(hd)
            k_head(hd)


def _inproj(layer, xs, mod_all, ind, rc, rs, params):
    nb, tt, _ = xs.shape
    tm = TM_PROJ
    tok = lambda width: pl.BlockSpec((1, tm, width), lambda b, t: (b, t, 0))
    rope = pl.BlockSpec((tm, LANES), lambda b, t: (t, 0))
    out_widths = (NA_WIDTH, NA_WIDTH, NA_WIDTH, MLA_HEADS * MLA_GROUP, MLA_HEADS * MLA_GROUP, MLA_WIDTH,
                  LRU_W, LRU_W)
    gmix, w, *rest = params
    return pl.pallas_call(
        _inproj_kernel,
        grid=(nb, tt // tm),
        in_specs=[
            tok(D_MODEL), *_mod_specs(mod_all, layer, nb),
            _layer_spec(gmix, layer), _layer_spec(w, layer), _const_spec(ind.shape),
            *[_layer_spec(a, layer) for a in rest],
            rope, rope,
        ],
        out_specs=[tok(wd) for wd in out_widths],
        out_shape=[jax.ShapeDtypeStruct((nb, tt, wd), BF16) for wd in out_widths],
        compiler_params=pltpu.CompilerParams(vmem_limit_bytes=VMEM_LIMIT),
        name="inproj",
    )(xs, mod_all, mod_all, gmix, w, ind, *rest, rc, rs)


NA_WIN = NA_KH * GRID_W
NA_TQ = 512
NA_ROWS_PER_STEP = NA_TQ // GRID_W


def _softmax_pv(scores, values):
    m = functools.reduce(jnp.maximum, [jnp.max(s, axis=-1, keepdims=True) for s in scores])
    ps = [jnp.exp2(s - m) for s in scores]
    l = functools.reduce(jnp.add, [jnp.sum(p, axis=-1, keepdims=True) for p in ps])
    o = functools.reduce(jnp.add, [_dot(p.astype(BF16), v) for p, v in zip(ps, values)])
    return o * (1.0 / l)


def _stack_heads(q):
    lo = lax.broadcasted_iota(jnp.int32, q.shape, 1) < NA_DH
    return jnp.concatenate([jnp.where(lo, q, 0), jnp.where(lo, 0, q)], axis=0)


def _unstack_heads(o):
    n = o.shape[0] // 2
    lo = lax.broadcasted_iota(jnp.int32, (n, LANES), 1) < NA_DH
    return jnp.where(lo, o[:n], o[n:])


def _na_kernel(q_ref, k_ref, v_ref, bias_ref, o_ref, *, with_ctx):
    i = pl.program_id(1)
    npair = NA_WIDTH // LANES

    def ctx_block():
        outs = []
        for j in range(npair):
            cols = slice(LANES * j, LANES * (j + 1))
            s_c = _dot_t(_stack_heads(q_ref[0, 0:CTX_LEN, cols]), k_ref[0, 0:CTX_LEN, cols])
            outs.append(_unstack_heads(_softmax_pv([s_c], [v_ref[0, 0:CTX_LEN, cols]])))
        o_ref[0, 0:CTX_LEN, :] = jnp.concatenate(outs, axis=-1).astype(BF16)

    def latent_rows(blk):
        ng = NA_ROWS_PER_STEP
        qrows = pl.ds(pl.multiple_of(CTX_LEN + blk * NA_TQ, CTX_LEN), NA_TQ)
        sq = 2 * GRID_W
        offs, wins = [], []
        for g in range(ng):
            r = blk * ng + g
            rs = jnp.clip(r - NA_KH // 2, 0, GRID_W // 2 - NA_KH)
            offs.append(rs - r + NA_KH - 1)
            wins.append(pl.ds(pl.multiple_of(CTX_LEN + rs * GRID_W, GRID_W), NA_WIN))
        lo = lax.broadcasted_iota(jnp.int32, (NA_TQ, LANES), 1) < NA_DH
        grp = lambda z, g: z[sq * g:sq * (g + 1)]

        def scores(j):
            cols = slice(LANES * j, LANES * (j + 1))
            q = q_ref[0, qrows, cols]
            q_lo, q_hi = jnp.where(lo, q, 0), jnp.where(lo, 0, q)
            qst = jnp.concatenate([part[GRID_W * g:GRID_W * (g + 1)] for g in range(ng) for part in (q_lo, q_hi)],
                                  axis=0)
            s_c = _dot_t(qst, k_ref[0, 0:CTX_LEN, cols])
            s_w = [_dot_t(grp(qst, g), k_ref[0, wins[g], cols])
                   + bias_ref[offs[g], 2 * j:2 * j + 2].reshape(sq, NA_WIN) for g in range(ng)]
            return s_c, s_w

        def numerators(s_c, s_w):
            pcs, pws, ls = [], [], []
            for g in range(ng):
                sc, sw = grp(s_c, g), s_w[g]
                m = jnp.maximum(jnp.max(sc, axis=-1, keepdims=True), jnp.max(sw, axis=-1, keepdims=True))
                pc, pw = jnp.exp2(sc - m), jnp.exp2(sw - m)
                ls.append(1.0 / (jnp.sum(pc, axis=-1, keepdims=True) + jnp.sum(pw, axis=-1, keepdims=True)))
                pcs.append(pc.astype(BF16))
                pws.append(pw.astype(BF16))
            return jnp.concatenate(pcs, axis=0), pws, ls

        def values(j, p_c, p_w, inv_l):
            cols = slice(LANES * j, LANES * (j + 1))
            o_c = _dot(p_c, v_ref[0, 0:CTX_LEN, cols])
            rows = [_unstack_heads((grp(o_c, g) + _dot(p_w[g], v_ref[0, wins[g], cols])) * inv_l[g])
                    for g in range(ng)]
            return jnp.concatenate(rows, axis=0)

        outs = []
        s_next = scores(0)
        for j in range(npair):
            s_cur = s_next
            if j + 1 < npair:
                s_next = scores(j + 1)
            outs.append(values(j, *numerators(*s_cur)))
        o_ref[0, qrows, :] = jnp.concatenate(outs, axis=-1).astype(BF16)

    if with_ctx:
        pl.when(i == 0)(ctx_block)
        pl.when(i > 0)(lambda: latent_rows(i - 1))
    else:
        latent_rows(i)


def _na_attention(layer, q, k, v, bias, with_ctx):
    nb, tt, _ = q.shape
    full = pl.BlockSpec((1, tt, NA_WIDTH), lambda b, i: (b, 0, 0))
    return pl.pallas_call(
        functools.partial(_na_kernel, with_ctx=with_ctx),
        grid=(nb, (tt - CTX_LEN) // NA_TQ + (1 if with_ctx else 0)),
        in_specs=[full, full, full, _layer_spec(bias, layer)],
        out_specs=full,
        out_shape=jax.ShapeDtypeStruct((nb, tt, NA_WIDTH), BF16),
        compiler_params=pltpu.CompilerParams(vmem_limit_bytes=VMEM_LIMIT),
        name="na_attention",
    )(q, k, v, bias)


def _na_bias_table(rpb):
    qc = np.arange(GRID_W)[:, None]
    kc = np.arange(GRID_W)[None, :]
    cs = np.clip(qc - NA_KW // 2, 0, GRID_W - NA_KW)
    valid = (kc >= cs) & (kc < cs + NA_KW)
    dc = np.clip(kc - qc + NA_KW - 1, 0, 2 * NA_KW - 2)
    onehot = (dc[:, :, None] == np.arange(2 * NA_KW - 1)).astype(np.float32)
    per_row = jnp.einsum('lhdc,qkc->lhdqk', rpb, onehot, precision=lax.Precision.HIGHEST) * LOG2E
    per_row = jnp.where(valid, per_row, NEG)
    tab = jnp.stack([per_row[:, :, off:off + NA_KH] for off in range(NA_KH)], axis=1)
    tab = jnp.transpose(tab, (0, 1, 2, 4, 3, 5))
    return tab.reshape(rpb.shape[0], NA_KH, NA_HEADS, GRID_W, NA_WIN).astype(F32)


MLA_TQ = 512


def _mla_kernel(q_ref, k_ref, v_ref, o_ref, *, with_ctx):
    i = pl.program_id(1)
    tt = k_ref.shape[1]

    def run(row0, nq, nk):
        rows = pl.ds(row0, nq)
        lo = lax.broadcasted_iota(jnp.int32, (nq, LANES), 1) < MLA_V

        def scores(hd):
            hs = slice(MLA_GROUP * hd, MLA_GROUP * (hd + 1))
            return _dot_t(q_ref[0, rows, hs], k_ref[0, 0:nk, hs])

        def values(hd, p, inv_l):
            vcols = slice(LANES * (hd // 2), LANES * (hd // 2 + 1))
            return _dot(p, v_ref[0, 0:nk, vcols]) * inv_l

        outs = []
        s_next = scores(0)
        for hd in range(MLA_HEADS):
            s = s_next
            if hd + 1 < MLA_HEADS:
                s_next = scores(hd + 1)
            p = jnp.exp2(s - jnp.max(s, axis=-1, keepdims=True))
            outs.append(values(hd, p.astype(BF16), 1.0 / jnp.sum(p, axis=-1, keepdims=True)))
        pairs = [jnp.where(lo, outs[2 * j], outs[2 * j + 1]) for j in range(MLA_WIDTH // LANES)]
        o_ref[0, rows, :] = jnp.concatenate(pairs, axis=-1).astype(BF16)

    latent_row0 = lambda blk: pl.multiple_of(CTX_LEN + blk * MLA_TQ, CTX_LEN)
    if with_ctx:
        pl.when(i == 0)(lambda: run(0, CTX_LEN, CTX_LEN))
        pl.when(i > 0)(lambda: run(latent_row0(i - 1), MLA_TQ, tt))
    else:
        run(latent_row0(i), MLA_TQ, tt)


def _mla_attention(q, k, v, with_ctx):
    nb, tt, _ = q.shape
    whole = lambda width: pl.BlockSpec((1, tt, width), lambda b, i: (b, 0, 0))
    return pl.pallas_call(
        functools.partial(_mla_kernel, with_ctx=with_ctx),
        grid=(nb, (tt - CTX_LEN) // MLA_TQ + (1 if with_ctx else 0)),
        in_specs=[whole(MLA_HEADS * MLA_GROUP), whole(MLA_HEADS * MLA_GROUP), whole(MLA_WIDTH)],
        out_specs=whole(MLA_WIDTH),
        out_shape=jax.ShapeDtypeStruct((nb, tt, MLA_WIDTH), BF16),
        compiler_params=pltpu.CompilerParams(vmem_limit_bytes=VMEM_LIMIT),
        name="mla_attention",
    )(q, k, v)


LRU_CHUNK = 256
LRU_TILES = LRU_CHUNK // SUBLANES


def _chunk_permutation():
    i = np.arange(LRU_CHUNK)
    p = np.zeros((LRU_CHUNK, LRU_CHUNK), np.float32)
    p[i, LRU_TILES * (i % SUBLANES) + i // SUBLANES] = 1.0
    return p


def _scan_tiles(a, b, carry, reverse):
    order = list(reversed(range(LRU_TILES))) if reverse else list(range(LRU_TILES))
    acum, bcum = [None] * LRU_TILES, [None] * LRU_TILES
    prev = None
    for k in order:
        if prev is None:
            acum[k], bcum[k] = a[k], b[k]
        else:
            acum[k] = a[k] * acum[prev]
            bcum[k] = a[k] * bcum[prev] + b[k]
        prev = k
    p_end, e_end = acum[prev], bcum[prev]
    sub = list(reversed(range(SUBLANES))) if reverse else list(range(SUBLANES))
    cin = [None] * SUBLANES
    for s in sub:
        cin[s] = carry
        carry = p_end[s:s + 1, :] * carry + e_end[s:s + 1, :]
    cin = jnp.concatenate(cin, axis=0)
    return [bcum[k] + acum[k] * cin for k in range(LRU_TILES)], carry


def _lru_kernel(lu_ref, lg_ref, perm_ref, permt_ref, cw_ref, cb_ref, wg_ref, bg_ref, lam_ref, o_ref,
                lp_ref, u_ref, hs_ref):
    tt = lu_ref.shape[1]
    nchunk = tt // LRU_CHUNK
    chunk_rows = lambda c: slice(c * LRU_CHUNK, (c + 1) * LRU_CHUNK)
    for c in range(nchunk):
        lp_ref[chunk_rows(c), :] = _dot(perm_ref[...], lu_ref[0, chunk_rows(c), :])

    left = CONV_W // 2
    sub = lax.broadcasted_iota(jnp.int32, (SUBLANES, LRU_W), 0)

    def conv_chunk(c, _):
        base = pl.multiple_of(c * LRU_CHUNK, LRU_CHUNK)
        tile = lambda b, k: lp_ref[pl.ds(b + SUBLANES * k, SUBLANES), :]
        prev_base = pl.multiple_of(jnp.maximum(c - 1, 0) * LRU_CHUNK, LRU_CHUNK)
        next_base = pl.multiple_of(jnp.minimum(c + 1, nchunk - 1) * LRU_CHUNK, LRU_CHUNK)
        has_prev = c >= 2
        has_next = (c >= 1) & (c < nchunk - 1)

        def earlier(k):
            edge = jnp.where(has_prev, tile(prev_base, k), 0.0)
            return pltpu.roll(jnp.where(sub == SUBLANES - 1, edge, tile(base, k)), 1, 0)

        def later(k):
            edge = jnp.where(has_next, tile(next_base, k), 0.0)
            return pltpu.roll(jnp.where(sub == 0, edge, tile(base, k)), SUBLANES - 1, 0)

        for k in range(LRU_TILES):
            acc = jnp.broadcast_to(cb_ref[...], (SUBLANES, LRU_W))
            for j in range(CONV_W):
                kk = k + j - left
                if kk < 0:
                    tap = earlier(kk + LRU_TILES)
                elif kk >= LRU_TILES:
                    tap = later(kk - LRU_TILES)
                else:
                    tap = tile(base, kk)
                acc = acc + tap * cw_ref[j:j + 1, :]
            u_ref[pl.ds(base + SUBLANES * k, SUBLANES), :] = acc
        return 0

    lax.fori_loop(0, nchunk, conv_chunk, 0)

    for d in (0, 1):
        reverse = d == 1
        nlam = -lam_ref[d]
        sp = jnp.maximum(nlam, 0.0) + jnp.log(1.0 + jnp.exp(-jnp.abs(nlam)))
        half_c = (0.5 * LRU_C) * sp

        def chunk_step(i, carry, d=d, reverse=reverse, half_c=half_c):
            c = jnp.where(i == 0, 0, nchunk - i) if reverse else i
            rows = pl.ds(pl.multiple_of(c * LRU_CHUNK, LRU_CHUNK), LRU_CHUNK)
            u = u_ref[rows, :]
            th = jnp.tanh(_dot(u.astype(BF16), wg_ref[d]) + bg_ref[d])
            neg_log_a = half_c * th[:, :LRU_W] + half_c
            a = jnp.exp2(neg_log_a * (-LOG2E))
            x = jnp.tanh(neg_log_a) * (a * a + 1.0)
            root = x * lax.rsqrt(jnp.maximum(x, F32_TINY))
            bterm = root * (th[:, LRU_W:] + 1.0) * (0.5 * u)
            split = lambda z: [z[SUBLANES * k:SUBLANES * (k + 1), :] for k in range(LRU_TILES)]
            h, carry = _scan_tiles(split(a), split(bterm), carry, reverse)
            h = jnp.concatenate(h, axis=0)
            hs_ref[rows, :] = hs_ref[rows, :] + h if reverse else h
            return carry

        lax.fori_loop(0, nchunk, chunk_step, jnp.zeros((1, LRU_W), F32))

    k0 = float(np.sqrt(2.0 / np.pi))
    def gate_tile_order(c):
        return _dot(perm_ref[...], lg_ref[0, chunk_rows(c), :])

    z_next = gate_tile_order(0)
    for c in range(nchunk):
        z = z_next
        if c + 1 < nchunk:
            z_next = gate_tile_order(c + 1)
        gelu = z * (0.5 * (1.0 + jnp.tanh(k0 * (z + 0.044715 * (z * z * z)))))
        o_tile_order = (gelu * hs_ref[chunk_rows(c), :]).astype(BF16)
        o_ref[0, chunk_rows(c), :] = _dot(permt_ref[...], o_tile_order).astype(BF16)


def _lru(layer, lu, lg, params):
    nb, tt, _ = lu.shape
    full = pl.BlockSpec((1, tt, LRU_W), lambda b: (b, 0, 0))
    perm = _chunk_permutation()
    perms = (jnp.asarray(perm, BF16), jnp.asarray(perm.T, BF16))
    consts = perms + tuple(params)
    return pl.pallas_call(
        _lru_kernel,
        grid=(nb,),
        in_specs=[full, full, *[_const_spec(a.shape) for a in perms], *[_layer_spec(a, layer) for a in params]],
        out_specs=full,
        out_shape=jax.ShapeDtypeStruct((nb, tt, LRU_W), BF16),
        scratch_shapes=[pltpu.VMEM((tt, LRU_W), F32)] * 3,
        compiler_params=pltpu.CompilerParams(vmem_limit_bytes=VMEM_LIMIT),
        name="rglru",
    )(lu, lg, *consts)


FF_CHUNK = 2048


def _mix_mlp_kernel(x_ref, ona_ref, omla_ref, olru_ref, modb_ref, modc_ref, gmix_ref, gmlp_ref, wgt_ref, wna_ref,
                    wmla_ref, wlru_ref, wo_ref, w1_ref, w2_ref, o_ref, *, t_off):
    tm = x_ref.shape[1]
    is_ctx = (lax.broadcasted_iota(jnp.int32, (tm, 1), 0) < CTX_LEN) & (pl.program_id(1) + t_off == 0)
    mod = _mod_rows(modb_ref, modc_ref, is_ctx)
    x = x_ref[0]
    h = _modulated_norm(x, gmix_ref[...], mod(0), mod(1)).astype(BF16)
    gates = _sigmoid(_dot(h, wgt_ref[...]))
    y = None
    for g, (o_r, w_r) in enumerate(((ona_ref, wna_ref), (omla_ref, wmla_ref), (olru_ref, wlru_ref))):
        term = gates[:, D_MODEL * g:D_MODEL * (g + 1)] * _dot(o_r[0], w_r[...])
        y = term if y is None else y + term
    x1 = x + mod(2) * _dot(y.astype(BF16), wo_ref[...])

    h2 = _modulated_norm(x1, gmlp_ref[...], mod(3), mod(4)).astype(BF16)
    acc = jnp.zeros_like(x1)
    for c in range(D_FF // FF_CHUNK):
        a = jnp.maximum(_dot(h2, w1_ref[:, FF_CHUNK * c:FF_CHUNK * (c + 1)]), 0.0)
        acc = acc + _dot((a * a).astype(BF16), w2_ref[FF_CHUNK * c:FF_CHUNK * (c + 1), :])
    o_ref[0] = x1 + mod(5) * acc


def _mix_mlp(layer, xs, ona, omla, olru, mod_all, params, with_ctx):
    nb, tt, _ = xs.shape
    tm = TM_MIX if with_ctx else TQ
    t_off = 0 if with_ctx else CTX_LEN // tm
    nt = tt // tm - t_off
    tok = lambda width: pl.BlockSpec((1, tm, width), lambda b, i: (b, i + t_off, 0))
    return pl.pallas_call(
        functools.partial(_mix_mlp_kernel, t_off=t_off),
        grid=(nb, nt),
        in_specs=[
            tok(D_MODEL), tok(NA_WIDTH), tok(MLA_WIDTH), tok(LRU_W), *_mod_specs(mod_all, layer, nb),
            *[_layer_spec(a, layer) for a in params],
        ],
        out_specs=pl.BlockSpec((1, tm, D_MODEL), lambda b, i: (b, i, 0)),
        out_shape=jax.ShapeDtypeStruct((nb, nt * tm, D_MODEL), F32),
        compiler_params=pltpu.CompilerParams(vmem_limit_bytes=VMEM_LIMIT),
        name="mix_mlp",
    )(xs, ona, omla, olru, mod_all, mod_all, *params)


def _group_source():
    nf = MLA_ROPE // 4
    half = LANES // 2
    src = np.full(LANES, -1)
    for part in (0, 1):
        lane0 = half * part
        src[lane0:lane0 + MLA_NOPE // 2] = MLA_NOPE // 2 * part + np.arange(MLA_NOPE // 2)
        src[lane0 + MLA_NOPE // 2:lane0 + MLA_NOPE // 2 + nf] = MLA_NOPE + nf * part + np.arange(nf)
        src[lane0 + MLA_NOPE // 2 + nf:lane0 + MLA_NOPE // 2 + 2 * nf] = MLA_NOPE + 2 * nf + nf * part + np.arange(nf)
    return src


_GROUP_SRC = _group_source()


def _to_group(v):
    return jnp.where(_GROUP_SRC >= 0, jnp.take(v, np.maximum(_GROUP_SRC, 0), axis=-1), 0)


def _pad_last(v, before, after):
    return jnp.pad(v, [(0, 0)] * (v.ndim - 1) + [(before, after)])


def _heads_to_groups(w, width):
    return _to_group(_pad_last(w, 0, MLA_QK - width)).reshape(*w.shape[:-2], MLA_HEADS * MLA_GROUP)


def _relayout_w_in(w):
    o_mr = 3 * NA_WIDTH + MLA_Q_RANK + MLA_KV_RANK
    o_gt = o_mr + MLA_ROPE + 2 * LRU_W
    mr_group = _to_group(_pad_last(w[..., o_mr:o_mr + MLA_ROPE], MLA_NOPE, 0))
    w_proj = jnp.concatenate([w[..., :o_mr], mr_group, w[..., o_mr + MLA_ROPE:o_gt]], axis=-1)
    return w_proj.astype(BF16), w[..., o_gt:].astype(BF16)


def _rope_tables(tt):
    p = np.arange(tt - CTX_LEN)
    nf = MLA_ROPE // 4
    inv = ROPE_BASE ** (-jnp.arange(nf, dtype=F32) / nf)
    cos, sin = [], []
    for pos in (jnp.asarray(p // GRID_W, F32), jnp.asarray(p % GRID_W, F32)):
        ang = pos[:, None] * inv
        cos += [jnp.cos(ang), jnp.cos(ang)]
        sin += [-jnp.sin(ang), jnp.sin(ang)]
    ones = jnp.ones((tt - CTX_LEN, MLA_NOPE), F32)
    rc = jnp.concatenate([jnp.ones((CTX_LEN, MLA_QK), F32), jnp.concatenate([ones] + cos, axis=1)], axis=0)
    rs = jnp.concatenate([jnp.zeros((CTX_LEN, MLA_QK), F32), jnp.concatenate([0.0 * ones] + sin, axis=1)], axis=0)
    return _to_group(rc), _to_group(rs)


def _block_diag(w):
    eye = jnp.eye(LRU_BLOCKS, dtype=w.dtype)
    return jnp.einsum('...ncd,nm->...ncmd', w, eye).reshape(*w.shape[:-3], LRU_W, LRU_W)


def kernel(x, c, ctx, c_ctx, w_mod, b_mod, g_mix, g_mlp, w_in, na_q_gain, na_k_gain, na_rpb, mla_qa_gain, w_q_b,
           mla_kva_gain, w_kv_b, mla_q_gain, mla_k_gain, lru_conv_w, lru_conv_b, lru_wa, lru_ba, lru_wx, lru_bx,
           lru_lambda, w_na_o, w_mla_o, w_lru_o, w_o, w_ff1, w_ff2):
    nb, seq, _ = x.shape
    depth = w_in.shape[0]
    assert ctx.shape[1] == CTX_LEN and seq // GRID_W == GRID_W // 2 and seq % GRID_W == 0
    tt = CTX_LEN + seq
    assert tt % TM_PROJ == 0 and tt % TM_MIX == 0 and tt % TQ == 0 and tt % LRU_CHUNK == 0

    xs = jnp.concatenate([ctx, x], axis=1)
    rp = -(-(nb + 1) // SUBLANES) * SUBLANES
    cvec = jnp.concatenate([c, c_ctx[None, :], jnp.zeros((rp - nb - 1, D_MODEL), F32)], axis=0)
    mod_all = _modulation(cvec, w_mod, b_mod).reshape(depth, rp, 6, D_MODEL)
    rc, rs = _rope_tables(tt)
    head_ind = jnp.asarray(np.kron(np.eye(NA_HEADS), np.ones((NA_DH, NA_DH))), BF16)

    rows = lambda v: v[:, None, :]
    bf = lambda v: v.astype(BF16)
    kvb = w_kv_b.reshape(depth, MLA_KV_RANK, MLA_HEADS, MLA_NOPE + MLA_V)
    w_proj, w_gate = _relayout_w_in(w_in)
    proj_params = (
        rows(g_mix), w_proj,
        rows(jnp.tile(na_q_gain, (1, NA_HEADS)) * (NA_DH ** -0.5 * LOG2E)), rows(jnp.tile(na_k_gain, (1, NA_HEADS))),
        rows(mla_qa_gain), bf(_heads_to_groups(w_q_b.reshape(depth, MLA_Q_RANK, MLA_HEADS, MLA_QK), MLA_QK)),
        rows(mla_kva_gain),
        bf(jnp.concatenate([_heads_to_groups(kvb[..., :MLA_NOPE], MLA_NOPE),
                            kvb[..., MLA_NOPE:].reshape(depth, MLA_KV_RANK, MLA_WIDTH)], axis=-1)),
        rows(_to_group(mla_q_gain) * (MLA_QK ** -0.5 * LOG2E)), rows(_to_group(mla_k_gain)))
    na_bias = _na_bias_table(na_rpb)
    lru_params = (
        lru_conv_w, rows(lru_conv_b),
        bf(0.5 * jnp.concatenate([_block_diag(lru_wa), _block_diag(lru_wx)], axis=-1)),
        0.5 * jnp.concatenate([lru_ba, lru_bx], axis=-1)[:, :, None, :], lru_lambda[:, :, None, :])
    mix_params = (rows(g_mix), rows(g_mlp), w_gate, bf(w_na_o), bf(w_mla_o), bf(w_lru_o), bf(w_o), bf(w_ff1),
                  bf(w_ff2))

    for i in range(depth):
        with_ctx = i < depth - 1
        naq, nak, nav, qm, km, vm, lu, lg = _inproj(i, xs, mod_all, head_ind, rc, rs, proj_params)
        o_na = _na_attention(i, naq, nak, nav, na_bias, with_ctx)
        o_mla = _mla_attention(qm, km, vm, with_ctx)
        o_lru = _lru(i, lu, lg, lru_params)
        xs = _mix_mlp(i, xs, o_na, o_mla, o_lru, mod_all, mix_params, with_ctx)
    return xs
```

```python
import functools

import numpy as np
import jax
import jax.numpy as jnp
from jax import lax
from jax.experimental import pallas as pl
from jax.experimental.pallas import tpu as pltpu

F32 = jnp.float32
BF16 = jnp.bfloat16

D_MODEL = 1024
CTX_LEN = 256
GRID_W = 64
NA_HEADS = 8
NA_DH = 64
NA_KH = 8
NA_KW = 16
NA_WIDTH = NA_HEADS * NA_DH
MLA_HEADS = 8
MLA_NOPE = 64
MLA_ROPE = 32
MLA_V = 64
MLA_QK = MLA_NOPE + MLA_ROPE
MLA_Q_RANK = 384
MLA_KV_RANK = 256
MLA_WIDTH = MLA_HEADS * MLA_V
LRU_W = 512
LRU_BLOCKS = 8
LRU_BS = LRU_W // LRU_BLOCKS
LRU_C = 8.0
CONV_W = 4
D_FF = 4 * D_MODEL
ROPE_BASE = 10000.0
EPS = 1e-6
NEG = -1e30
LOG2E = float(np.log2(np.e))
F32_TINY = float(np.finfo(np.float32).tiny)

LANES = 128
SUBLANES = 8
TQ = 256
TM_PROJ = 768
TM_MIX = 576
MLA_GROUP = LANES
VMEM_LIMIT = 56 * 1024 * 1024

C_NAQ, C_NAK, C_NAV = 0, 512, 1024
C_MQ = 1536
C_MKV = C_MQ + MLA_Q_RANK
C_MR = C_MKV + MLA_KV_RANK
C_LU = C_MR + LANES
C_LG = C_LU + LRU_W
C_END = C_LG + LRU_W


def _sigmoid(z):
    return 0.5 * jnp.tanh(0.5 * z) + 0.5


def _dot(a, b):
    return jnp.dot(a, b, preferred_element_type=F32)


def _dot_t(a, b):
    return lax.dot_general(a, b, (((1,), (1,)), ((), ())), preferred_element_type=F32)


def _const_spec(shape):
    nd = len(shape)
    return pl.BlockSpec(shape, lambda *_: (0,) * nd, pipeline_mode=pl.Buffered(1))


def _layer_spec(arr, layer):
    nd = arr.ndim
    return pl.BlockSpec((None,) + arr.shape[1:], lambda *_: (layer,) + (0,) * (nd - 1), pipeline_mode=pl.Buffered(1))


def _mod_specs(mod_all, layer, nb):
    shape = (None, 1) + mod_all.shape[2:]
    return [pl.BlockSpec(shape, lambda b, t: (layer, b, 0, 0)), pl.BlockSpec(shape, lambda b, t: (layer, nb, 0, 0))]


def _modulated_norm(x, gain, shift, scale):
    ms = jnp.mean(x * x, axis=-1, keepdims=True)
    return (x * lax.rsqrt(ms + EPS) * gain) * (1.0 + scale) + shift


def _mod_rows(modb_ref, modc_ref, is_ctx):
    return lambda k: jnp.where(is_ctx, modc_ref[0, k:k + 1, :], modb_ref[0, k:k + 1, :])


def _mod_kernel(c_ref, w_ref, b_ref, o_ref):
    cv = c_ref[...]
    s = cv * _sigmoid(cv)
    o_ref[0] = _dot(s.astype(BF16), w_ref[0].astype(BF16)) + b_ref[0]


def _modulation(cvec, w_mod, b_mod):
    depth = w_mod.shape[0]
    rp = cvec.shape[0]
    return pl.pallas_call(
        _mod_kernel,
        grid=(depth, 6),
        in_specs=[
            pl.BlockSpec((rp, D_MODEL), lambda i, j: (0, 0)),
            pl.BlockSpec((1, D_MODEL, D_MODEL), lambda i, j: (i, 0, j)),
            pl.BlockSpec((1, 1, D_MODEL), lambda i, j: (i, 0, j)),
        ],
        out_specs=pl.BlockSpec((1, rp, D_MODEL), lambda i, j: (i, 0, j)),
        out_shape=jax.ShapeDtypeStruct((depth, rp, 6 * D_MODEL), F32),
        name="modulation",
    )(cvec, w_mod, b_mod.reshape(depth, 1, 6 * D_MODEL))


def _head_norm64(p, ind, gain):
    ssq = _dot((p * p).astype(BF16), ind)
    return p * lax.rsqrt(ssq * (1.0 / NA_DH) + EPS) * gain


def _rope(n, rc, rs):
    return n * rc + pltpu.roll(n, LANES // 2, 1) * rs


def _inproj_kernel(x_ref, modb_ref, modc_ref, gmix_ref, w_ref, ind_ref, naqg_ref, nakg_ref, qag_ref, wqb_ref,
                   kvag_ref, wkv_ref, mqg_ref, mkg_ref, rc_ref, rs_ref,
                   naq_ref, nak_ref, nav_ref, qm_ref, km_ref, vm_ref, lu_ref, lg_ref):
    tm = x_ref.shape[1]
    is_ctx = (lax.broadcasted_iota(jnp.int32, (tm, 1), 0) < CTX_LEN) & (pl.program_id(1) == 0)
    mod = _mod_rows(modb_ref, modc_ref, is_ctx)
    h = _modulated_norm(x_ref[0], gmix_ref[...], mod(0), mod(1)).astype(BF16)

    def proj(a, b):
        return _dot(h, w_ref[:, a:b])

    rc, rs = rc_ref[...], rs_ref[...]

    platent = proj(C_MQ, C_LU)
    pq = platent[:, :MLA_Q_RANK]
    pkv = platent[:, MLA_Q_RANK:MLA_Q_RANK + MLA_KV_RANK]
    kr = platent[:, MLA_Q_RANK + MLA_KV_RANK:]

    nq = pq * lax.rsqrt(jnp.mean(pq * pq, axis=-1, keepdims=True) + EPS) * qag_ref[...]
    q0 = _dot(nq.astype(BF16), wqb_ref[...])

    nkv = (pkv * lax.rsqrt(jnp.mean(pkv * pkv, axis=-1, keepdims=True) + EPS) * kvag_ref[...]).astype(BF16)
    kv = _dot(nkv, wkv_ref[...])
    k0 = kv[:, :MLA_HEADS * MLA_GROUP]
    vm_ref[0] = kv[:, MLA_HEADS * MLA_GROUP:].astype(BF16)
    kr_ss = jnp.sum(kr * kr, axis=-1, keepdims=True)
    kr_rot = _rope(kr * mkg_ref[...], rc, rs)

    def q_head(hd):
        sl = slice(MLA_GROUP * hd, MLA_GROUP * (hd + 1))
        blk = q0[:, sl]
        ss = jnp.sum(blk * blk, axis=-1, keepdims=True) * (1.0 / MLA_QK)
        qm_ref[0, :, sl] = _rope(blk * lax.rsqrt(ss + EPS) * mqg_ref[...], rc, rs).astype(BF16)

    def k_head(hd):
        sl = slice(MLA_GROUP * hd, MLA_GROUP * (hd + 1))
        blk = k0[:, sl]
        ss = (jnp.sum(blk * blk, axis=-1, keepdims=True) + kr_ss) * (1.0 / MLA_QK)
        km_ref[0, :, sl] = ((blk * mkg_ref[...] + kr_rot) * lax.rsqrt(ss + EPS)).astype(BF16)

    def wide(i):
        if i == 0:
            naq_ref[0] = _head_norm64(proj(C_NAQ, C_NAK), ind_ref[...], naqg_ref[...]).astype(BF16)
        elif i == 1:
            nak_ref[0] = _head_norm64(proj(C_NAK, C_NAV), ind_ref[...], nakg_ref[...]).astype(BF16)
        elif i == 2:
            nav_ref[0] = proj(C_NAV, C_MQ).astype(BF16)
        elif i == 3:
            lu_ref[0] = proj(C_LU, C_LG).astype(BF16)
        else:
            lg_ref[0] = proj(C_LG, C_END).astype(BF16)

    for i in range(5):
        wide(i)
        for hd in range(2 * i, min(2 * i + 2, MLA_HEADS)):
            q_head(hd)
            k_head(hd)


def _inproj(layer, xs, mod_all, ind, rc, rs, params):
    nb, tt, _ = xs.shape
    tm = TM_PROJ
    tok = lambda width: pl.BlockSpec((1, tm, width), lambda b, t: (b, t, 0))
    rope = pl.BlockSpec((tm, LANES), lambda b, t: (t, 0))
    out_widths = (NA_WIDTH, NA_WIDTH, NA_WIDTH, MLA_HEADS * MLA_GROUP, MLA_HEADS * MLA_GROUP, MLA_WIDTH,
                  LRU_W, LRU_W)
    gmix, w, *rest = params
    return pl.pallas_call(
        _inproj_kernel,
        grid=(nb, tt // tm),
        in_specs=[
            tok(D_MODEL), *_mod_specs(mod_all, layer, nb),
            _layer_spec(gmix, layer), _layer_spec(w, layer), _const_spec(ind.shape),
            *[_layer_spec(a, layer) for a in rest],
            rope, rope,
        ],
        out_specs=[tok(wd) for wd in out_widths],
        out_shape=[jax.ShapeDtypeStruct((nb, tt, wd), BF16) for wd in out_widths],
        compiler_params=pltpu.CompilerParams(vmem_limit_bytes=VMEM_LIMIT),
        name="inproj",
    )(xs, mod_all, mod_all, gmix, w, ind, *rest, rc, rs)


NA_WIN = NA_KH * GRID_W
NA_TQ = 512
NA_ROWS_PER_STEP = NA_TQ // GRID_W


def _softmax_pv(scores, values):
    m = functools.reduce(jnp.maximum, [jnp.max(s, axis=-1, keepdims=True) for s in scores])
    ps = [jnp.exp2(s - m) for s in scores]
    l = functools.reduce(jnp.add, [jnp.sum(p, axis=-1, keepdims=True) for p in ps])
    o = functools.reduce(jnp.add, [_dot(p.astype(BF16), v) for p, v in zip(ps, values)])
    return o * (1.0 / l)


def _stack_heads(q):
    lo = lax.broadcasted_iota(jnp.int32, q.shape, 1) < NA_DH
    return jnp.concatenate([jnp.where(lo, q, 0), jnp.where(lo, 0, q)], axis=0)


def _unstack_heads(o):
    n = o.shape[0] // 2
    lo = lax.broadcasted_iota(jnp.int32, (n, LANES), 1) < NA_DH
    return jnp.where(lo, o[:n], o[n:])


def _na_kernel(q_ref, k_ref, v_ref, bias_ref, o_ref, *, with_ctx):
    i = pl.program_id(1)
    npair = NA_WIDTH // LANES

    def ctx_block():
        outs = []
        for j in range(npair):
            cols = slice(LANES * j, LANES * (j + 1))
            s_c = _dot_t(_stack_heads(q_ref[0, 0:CTX_LEN, cols]), k_ref[0, 0:CTX_LEN, cols])
            outs.append(_unstack_heads(_softmax_pv([s_c], [v_ref[0, 0:CTX_LEN, cols]])))
        o_ref[0, 0:CTX_LEN, :] = jnp.concatenate(outs, axis=-1).astype(BF16)

    def latent_rows(blk):
        ng = NA_ROWS_PER_STEP
        qrows = pl.ds(pl.multiple_of(CTX_LEN + blk * NA_TQ, CTX_LEN), NA_TQ)
        sq = 2 * GRID_W
        offs, wins = [], []
        for g in range(ng):
            r = blk * ng + g
            rs = jnp.clip(r - NA_KH // 2, 0, GRID_W // 2 - NA_KH)
            offs.append(rs - r + NA_KH - 1)
            wins.append(pl.ds(pl.multiple_of(CTX_LEN + rs * GRID_W, GRID_W), NA_WIN))
        lo = lax.broadcasted_iota(jnp.int32, (NA_TQ, LANES), 1) < NA_DH
        grp = lambda z, g: z[sq * g:sq * (g + 1)]

        def scores(j):
            cols = slice(LANES * j, LANES * (j + 1))
            q = q_ref[0, qrows, cols]
            q_lo, q_hi = jnp.where(lo, q, 0), jnp.where(lo, 0, q)
            qst = jnp.concatenate([part[GRID_W * g:GRID_W * (g + 1)] for g in range(ng) for part in (q_lo, q_hi)],
                                  axis=0)
            s_c = _dot_t(qst, k_ref[0, 0:CTX_LEN, cols])
            s_w = [_dot_t(grp(qst, g), k_ref[0, wins[g], cols])
                   + bias_ref[offs[g], 2 * j:2 * j + 2].reshape(sq, NA_WIN) for g in range(ng)]
            return s_c, s_w

        def numerators(s_c, s_w):
            pcs, pws, ls = [], [], []
            for g in range(ng):
                sc, sw = grp(s_c, g), s_w[g]
                m = jnp.maximum(jnp.max(sc, axis=-1, keepdims=True), jnp.max(sw, axis=-1, keepdims=True))
                pc, pw = jnp.exp2(sc - m), jnp.exp2(sw - m)
                ls.append(1.0 / (jnp.sum(pc, axis=-1, keepdims=True) + jnp.sum(pw, axis=-1, keepdims=True)))
                pcs.append(pc.astype(BF16))
                pws.append(pw.astype(BF16))
            return jnp.concatenate(pcs, axis=0), pws, ls

        def values(j, p_c, p_w, inv_l):
            cols = slice(LANES * j, LANES * (j + 1))
            o_c = _dot(p_c, v_ref[0, 0:CTX_LEN, cols])
            rows = [_unstack_heads((grp(o_c, g) + _dot(p_w[g], v_ref[0, wins[g], cols])) * inv_l[g])
                    for g in range(ng)]
            return jnp.concatenate(rows, axis=0)

        outs = []
        s_next = scores(0)
        for j in range(npair):
            s_cur = s_next
            if j + 1 < npair:
                s_next = scores(j + 1)
            outs.append(values(j, *numerators(*s_cur)))
        o_ref[0, qrows, :] = jnp.concatenate(outs, axis=-1).astype(BF16)

    if with_ctx:
        pl.when(i == 0)(ctx_block)
        pl.when(i > 0)(lambda: latent_rows(i - 1))
    else:
        latent_rows(i)


def _na_attention(layer, q, k, v, bias, with_ctx):
    nb, tt, _ = q.shape
    full = pl.BlockSpec((1, tt, NA_WIDTH), lambda b, i: (b, 0, 0))
    return pl.pallas_call(
        functools.partial(_na_kernel, with_ctx=with_ctx),
        grid=(nb, (tt - CTX_LEN) // NA_TQ + (1 if with_ctx else 0)),
        in_specs=[full, full, full, _layer_spec(bias, layer)],
        out_specs=full,
        out_shape=jax.ShapeDtypeStruct((nb, tt, NA_WIDTH), BF16),
        compiler_params=pltpu.CompilerParams(vmem_limit_bytes=VMEM_LIMIT),
        name="na_attention",
    )(q, k, v, bias)


def _na_bias_table(rpb):
    qc = np.arange(GRID_W)[:, None]
    kc = np.arange(GRID_W)[None, :]
    cs = np.clip(qc - NA_KW // 2, 0, GRID_W - NA_KW)
    valid = (kc >= cs) & (kc < cs + NA_KW)
    dc = np.clip(kc - qc + NA_KW - 1, 0, 2 * NA_KW - 2)
    onehot = (dc[:, :, None] == np.arange(2 * NA_KW - 1)).astype(np.float32)
    per_row = jnp.einsum('lhdc,qkc->lhdqk', rpb, onehot, precision=lax.Precision.HIGHEST) * LOG2E
    per_row = jnp.where(valid, per_row, NEG)
    tab = jnp.stack([per_row[:, :, off:off + NA_KH] for off in range(NA_KH)], axis=1)
    tab = jnp.transpose(tab, (0, 1, 2, 4, 3, 5))
    return tab.reshape(rpb.shape[0], NA_KH, NA_HEADS, GRID_W, NA_WIN).astype(F32)


MLA_TQ = 512


def _mla_kernel(q_ref, k_ref, v_ref, o_ref, *, with_ctx):
    i = pl.program_id(1)
    tt = k_ref.shape[1]

    def run(row0, nq, nk):
        rows = pl.ds(row0, nq)
        lo = lax.broadcasted_iota(jnp.int32, (nq, LANES), 1) < MLA_V

        def scores(hd):
            hs = slice(MLA_GROUP * hd, MLA_GROUP * (hd + 1))
            return _dot_t(q_ref[0, rows, hs], k_ref[0, 0:nk, hs])

        def values(hd, p, inv_l):
            vcols = slice(LANES * (hd // 2), LANES * (hd // 2 + 1))
            return _dot(p, v_ref[0, 0:nk, vcols]) * inv_l

        outs = []
        s_next = scores(0)
        for hd in range(MLA_HEADS):
            s = s_next
            if hd + 1 < MLA_HEADS:
                s_next = scores(hd + 1)
            p = jnp.exp2(s - jnp.max(s, axis=-1, keepdims=True))
            outs.append(values(hd, p.astype(BF16), 1.0 / jnp.sum(p, axis=-1, keepdims=True)))
        pairs = [jnp.where(lo, outs[2 * j], outs[2 * j + 1]) for j in range(MLA_WIDTH // LANES)]
        o_ref[0, rows, :] = jnp.concatenate(pairs, axis=-1).astype(BF16)

    latent_row0 = lambda blk: pl.multiple_of(CTX_LEN + blk * MLA_TQ, CTX_LEN)
    if with_ctx:
        pl.when(i == 0)(lambda: run(0, CTX_LEN, CTX_LEN))
        pl.when(i > 0)(lambda: run(latent_row0(i - 1), MLA_TQ, tt))
    else:
        run(latent_row0(i), MLA_TQ, tt)


def _mla_attention(q, k, v, with_ctx):
    nb, tt, _ = q.shape
    whole = lambda width: pl.BlockSpec((1, tt, width), lambda b, i: (b, 0, 0))
    return pl.pallas_call(
        functools.partial(_mla_kernel, with_ctx=with_ctx),
        grid=(nb, (tt - CTX_LEN) // MLA_TQ + (1 if with_ctx else 0)),
        in_specs=[whole(MLA_HEADS * MLA_GROUP), whole(MLA_HEADS * MLA_GROUP), whole(MLA_WIDTH)],
        out_specs=whole(MLA_WIDTH),
        out_shape=jax.ShapeDtypeStruct((nb, tt, MLA_WIDTH), BF16),
        compiler_params=pltpu.CompilerParams(vmem_limit_bytes=VMEM_LIMIT),
        name="mla_attention",
    )(q, k, v)


LRU_CHUNK = 256
LRU_TILES = LRU_CHUNK // SUBLANES


def _chunk_permutation():
    i = np.arange(LRU_CHUNK)
    p = np.zeros((LRU_CHUNK, LRU_CHUNK), np.float32)
    p[i, LRU_TILES * (i % SUBLANES) + i // SUBLANES] = 1.0
    return p


def _scan_tiles(a, b, carry, reverse):
    order = list(reversed(range(LRU_TILES))) if reverse else list(range(LRU_TILES))
    acum, bcum = [None] * LRU_TILES, [None] * LRU_TILES
    prev = None
    for k in order:
        if prev is None:
            acum[k], bcum[k] = a[k], b[k]
        else:
            acum[k] = a[k] * acum[prev]
            bcum[k] = a[k] * bcum[prev] + b[k]
        prev = k
    p_end, e_end = acum[prev], bcum[prev]
    sub = list(reversed(range(SUBLANES))) if reverse else list(range(SUBLANES))
    cin = [None] * SUBLANES
    for s in sub:
        cin[s] = carry
        carry = p_end[s:s + 1, :] * carry + e_end[s:s + 1, :]
    cin = jnp.concatenate(cin, axis=0)
    return [bcum[k] + acum[k] * cin for k in range(LRU_TILES)], carry


def _lru_kernel(lu_ref, lg_ref, perm_ref, permt_ref, cw_ref, cb_ref, wg_ref, bg_ref, lam_ref, o_ref,
                lp_ref, u_ref, hs_ref, hb_ref):
    tt = lu_ref.shape[1]
    nchunk = tt // LRU_CHUNK
    chunk_rows = lambda c: slice(c * LRU_CHUNK, (c + 1) * LRU_CHUNK)
    for c in range(nchunk):
        lp_ref[chunk_rows(c), :] = _dot(perm_ref[...], lu_ref[0, chunk_rows(c), :])

    left = CONV_W // 2
    sub = lax.broadcasted_iota(jnp.int32, (SUBLANES, LRU_W), 0)

    def conv_chunk(c, _):
        base = pl.multiple_of(c * LRU_CHUNK, LRU_CHUNK)
        tile = lambda b, k: lp_ref[pl.ds(b + SUBLANES * k, SUBLANES), :]
        prev_base = pl.multiple_of(jnp.maximum(c - 1, 0) * LRU_CHUNK, LRU_CHUNK)
        next_base = pl.multiple_of(jnp.minimum(c + 1, nchunk - 1) * LRU_CHUNK, LRU_CHUNK)
        has_prev = c >= 2
        has_next = (c >= 1) & (c < nchunk - 1)

        def earlier(k):
            edge = jnp.where(has_prev, tile(prev_base, k), 0.0)
            return pltpu.roll(jnp.where(sub == SUBLANES - 1, edge, tile(base, k)), 1, 0)

        def later(k):
            edge = jnp.where(has_next, tile(next_base, k), 0.0)
            return pltpu.roll(jnp.where(sub == 0, edge, tile(base, k)), SUBLANES - 1, 0)

        for k in range(LRU_TILES):
            acc = jnp.broadcast_to(cb_ref[...], (SUBLANES, LRU_W))
            for j in range(CONV_W):
                kk = k + j - left
                if kk < 0:
                    tap = earlier(kk + LRU_TILES)
                elif kk >= LRU_TILES:
                    tap = later(kk - LRU_TILES)
                else:
                    tap = tile(base, kk)
                acc = acc + tap * cw_ref[j:j + 1, :]
            u_ref[pl.ds(base + SUBLANES * k, SUBLANES), :] = acc
        return 0

    lax.fori_loop(0, nchunk, conv_chunk, 0)

    def half_decay_scale(d):
        nlam = -lam_ref[d]
        sp = jnp.maximum(nlam, 0.0) + jnp.log(1.0 + jnp.exp(-jnp.abs(nlam)))
        return (0.5 * LRU_C) * sp

    half_c = (half_decay_scale(0), half_decay_scale(1))

    def chunk_scan(d, c, carry, h_ref):
        reverse = d == 1
        rows = pl.ds(pl.multiple_of(c * LRU_CHUNK, LRU_CHUNK), LRU_CHUNK)
        u = u_ref[rows, :]
        th = jnp.tanh(_dot(u.astype(BF16), wg_ref[d]) + bg_ref[d])
        neg_log_a = half_c[d] * th[:, :LRU_W] + half_c[d]
        a = jnp.exp2(neg_log_a * (-LOG2E))
        x = jnp.tanh(neg_log_a) * (a * a + 1.0)
        root = x * lax.rsqrt(jnp.maximum(x, F32_TINY))
        bterm = root * (th[:, LRU_W:] + 1.0) * (0.5 * u)
        split = lambda z: [z[SUBLANES * k:SUBLANES * (k + 1), :] for k in range(LRU_TILES)]
        h, carry = _scan_tiles(split(a), split(bterm), carry, reverse)
        h_ref[rows, :] = jnp.concatenate(h, axis=0)
        return carry

    def both_directions(i, carries):
        fwd = chunk_scan(0, i, carries[0], hs_ref)
        bwd = chunk_scan(1, jnp.where(i == 0, 0, nchunk - i), carries[1], hb_ref)
        return fwd, bwd

    zero = jnp.zeros((1, LRU_W), F32)
    lax.fori_loop(0, nchunk, both_directions, (zero, zero))

    k0 = float(np.sqrt(2.0 / np.pi))
    def gate_tile_order(c):
        return _dot(perm_ref[...], lg_ref[0, chunk_rows(c), :])

    z_next = gate_tile_order(0)
    for c in range(nchunk):
        z = z_next
        if c + 1 < nchunk:
            z_next = gate_tile_order(c + 1)
        gelu = z * (0.5 * (1.0 + jnp.tanh(k0 * (z + 0.044715 * (z * z * z)))))
        o_tile_order = (gelu * (hs_ref[chunk_rows(c), :] + hb_ref[chunk_rows(c), :])).astype(BF16)
        o_ref[0, chunk_rows(c), :] = _dot(permt_ref[...], o_tile_order).astype(BF16)


def _lru(layer, lu, lg, params):
    nb, tt, _ = lu.shape
    full = pl.BlockSpec((1, tt, LRU_W), lambda b: (b, 0, 0))
    perm = _chunk_permutation()
    perms = (jnp.asarray(perm, BF16), jnp.asarray(perm.T, BF16))
    consts = perms + tuple(params)
    return pl.pallas_call(
        _lru_kernel,
        grid=(nb,),
        in_specs=[full, full, *[_const_spec(a.shape) for a in perms], *[_layer_spec(a, layer) for a in params]],
        out_specs=full,
        out_shape=jax.ShapeDtypeStruct((nb, tt, LRU_W), BF16),
        scratch_shapes=[pltpu.VMEM((tt, LRU_W), F32)] * 4,
        compiler_params=pltpu.CompilerParams(vmem_limit_bytes=VMEM_LIMIT),
        name="rglru",
    )(lu, lg, *consts)


FF_CHUNK = 2048


def _mix_mlp_kernel(x_ref, ona_ref, omla_ref, olru_ref, modb_ref, modc_ref, gmix_ref, gmlp_ref, wgt_ref, wna_ref,
                    wmla_ref, wlru_ref, wo_ref, w1_ref, w2_ref, o_ref, *, t_off):
    tm = x_ref.shape[1]
    is_ctx = (lax.broadcasted_iota(jnp.int32, (tm, 1), 0) < CTX_LEN) & (pl.program_id(1) + t_off == 0)
    mod = _mod_rows(modb_ref, modc_ref, is_ctx)
    x = x_ref[0]
    h = _modulated_norm(x, gmix_ref[...], mod(0), mod(1)).astype(BF16)
    gates = _sigmoid(_dot(h, wgt_ref[...]))
    y = None
    for g, (o_r, w_r) in enumerate(((ona_ref, wna_ref), (omla_ref, wmla_ref), (olru_ref, wlru_ref))):
        term = gates[:, D_MODEL * g:D_MODEL * (g + 1)] * _dot(o_r[0], w_r[...])
        y = term if y is None else y + term
    x1 = x + mod(2) * _dot(y.astype(BF16), wo_ref[...])

    h2 = _modulated_norm(x1, gmlp_ref[...], mod(3), mod(4)).astype(BF16)
    acc = jnp.zeros_like(x1)
    for c in range(D_FF // FF_CHUNK):
        a = jnp.maximum(_dot(h2, w1_ref[:, FF_CHUNK * c:FF_CHUNK * (c + 1)]), 0.0)
        acc = acc + _dot((a * a).astype(BF16), w2_ref[FF_CHUNK * c:FF_CHUNK * (c + 1), :])
    o_ref[0] = x1 + mod(5) * acc


def _mix_mlp(layer, xs, ona, omla, olru, mod_all, params, with_ctx):
    nb, tt, _ = xs.shape
    tm = TM_MIX if with_ctx else TQ
    t_off = 0 if with_ctx else CTX_LEN // tm
    nt = tt // tm - t_off
    tok = lambda width: pl.BlockSpec((1, tm, width), lambda b, i: (b, i + t_off, 0))
    return pl.pallas_call(
        functools.partial(_mix_mlp_kernel, t_off=t_off),
        grid=(nb, nt),
        in_specs=[
            tok(D_MODEL), tok(NA_WIDTH), tok(MLA_WIDTH), tok(LRU_W), *_mod_specs(mod_all, layer, nb),
            *[_layer_spec(a, layer) for a in params],
        ],
        out_specs=pl.BlockSpec((1, tm, D_MODEL), lambda b, i: (b, i, 0)),
        out_shape=jax.ShapeDtypeStruct((nb, nt * tm, D_MODEL), F32),
        compiler_params=pltpu.CompilerParams(vmem_limit_bytes=VMEM_LIMIT),
        name="mix_mlp",
    )(xs, ona, omla, olru, mod_all, mod_all, *params)


def _group_source():
    nf = MLA_ROPE // 4
    half = LANES // 2
    src = np.full(LANES, -1)
    for part in (0, 1):
        lane0 = half * part
        src[lane0:lane0 + MLA_NOPE // 2] = MLA_NOPE // 2 * part + np.arange(MLA_NOPE // 2)
        src[lane0 + MLA_NOPE // 2:lane0 + MLA_NOPE // 2 + nf] = MLA_NOPE + nf * part + np.arange(nf)
        src[lane0 + MLA_NOPE // 2 + nf:lane0 + MLA_NOPE // 2 + 2 * nf] = MLA_NOPE + 2 * nf + nf * part + np.arange(nf)
    return src


_GROUP_SRC = _group_source()


def _to_group(v):
    return jnp.where(_GROUP_SRC >= 0, jnp.take(v, np.maximum(_GROUP_SRC, 0), axis=-1), 0)


def _pad_last(v, before, after):
    return jnp.pad(v, [(0, 0)] * (v.ndim - 1) + [(before, after)])


def _heads_to_groups(w, width):
    return _to_group(_pad_last(w, 0, MLA_QK - width)).reshape(*w.shape[:-2], MLA_HEADS * MLA_GROUP)


def _relayout_w_in(w):
    o_mr = 3 * NA_WIDTH + MLA_Q_RANK + MLA_KV_RANK
    o_gt = o_mr + MLA_ROPE + 2 * LRU_W
    mr_group = _to_group(_pad_last(w[..., o_mr:o_mr + MLA_ROPE], MLA_NOPE, 0))
    w_proj = jnp.concatenate([w[..., :o_mr], mr_group, w[..., o_mr + MLA_ROPE:o_gt]], axis=-1)
    return w_proj.astype(BF16), w[..., o_gt:].astype(BF16)


def _rope_tables(tt):
    p = np.arange(tt - CTX_LEN)
    nf = MLA_ROPE // 4
    inv = ROPE_BASE ** (-jnp.arange(nf, dtype=F32) / nf)
    cos, sin = [], []
    for pos in (jnp.asarray(p // GRID_W, F32), jnp.asarray(p % GRID_W, F32)):
        ang = pos[:, None] * inv
        cos += [jnp.cos(ang), jnp.cos(ang)]
        sin += [-jnp.sin(ang), jnp.sin(ang)]
    ones = jnp.ones((tt - CTX_LEN, MLA_NOPE), F32)
    rc = jnp.concatenate([jnp.ones((CTX_LEN, MLA_QK), F32), jnp.concatenate([ones] + cos, axis=1)], axis=0)
    rs = jnp.concatenate([jnp.zeros((CTX_LEN, MLA_QK), F32), jnp.concatenate([0.0 * ones] + sin, axis=1)], axis=0)
    return _to_group(rc), _to_group(rs)


def _block_diag(w):
    eye = jnp.eye(LRU_BLOCKS, dtype=w.dtype)
    return jnp.einsum('...ncd,nm->...ncmd', w, eye).reshape(*w.shape[:-3], LRU_W, LRU_W)


def kernel(x, c, ctx, c_ctx, w_mod, b_mod, g_mix, g_mlp, w_in, na_q_gain, na_k_gain, na_rpb, mla_qa_gain, w_q_b,
           mla_kva_gain, w_kv_b, mla_q_gain, mla_k_gain, lru_conv_w, lru_conv_b, lru_wa, lru_ba, lru_wx, lru_bx,
           lru_lambda, w_na_o, w_mla_o, w_lru_o, w_o, w_ff1, w_ff2):
    nb, seq, _ = x.shape
    depth = w_in.shape[0]
    assert ctx.shape[1] == CTX_LEN and seq // GRID_W == GRID_W // 2 and seq % GRID_W == 0
    tt = CTX_LEN + seq
    assert tt % TM_PROJ == 0 and tt % TM_MIX == 0 and tt % TQ == 0 and tt % LRU_CHUNK == 0

    xs = jnp.concatenate([ctx, x], axis=1)
    rp = -(-(nb + 1) // SUBLANES) * SUBLANES
    cvec = jnp.concatenate([c, c_ctx[None, :], jnp.zeros((rp - nb - 1, D_MODEL), F32)], axis=0)
    mod_all = _modulation(cvec, w_mod, b_mod).reshape(depth, rp, 6, D_MODEL)
    rc, rs = _rope_tables(tt)
    head_ind = jnp.asarray(np.kron(np.eye(NA_HEADS), np.ones((NA_DH, NA_DH))), BF16)

    rows = lambda v: v[:, None, :]
    bf = lambda v: v.astype(BF16)
    kvb = w_kv_b.reshape(depth, MLA_KV_RANK, MLA_HEADS, MLA_NOPE + MLA_V)
    w_proj, w_gate = _relayout_w_in(w_in)
    proj_params = (
        rows(g_mix), w_proj,
        rows(jnp.tile(na_q_gain, (1, NA_HEADS)) * (NA_DH ** -0.5 * LOG2E)), rows(jnp.tile(na_k_gain, (1, NA_HEADS))),
        rows(mla_qa_gain), bf(_heads_to_groups(w_q_b.reshape(depth, MLA_Q_RANK, MLA_HEADS, MLA_QK), MLA_QK)),
        rows(mla_kva_gain),
        bf(jnp.concatenate([_heads_to_groups(kvb[..., :MLA_NOPE], MLA_NOPE),
                            kvb[..., MLA_NOPE:].reshape(depth, MLA_KV_RANK, MLA_WIDTH)], axis=-1)),
        rows(_to_group(mla_q_gain) * (MLA_QK ** -0.5 * LOG2E)), rows(_to_group(mla_k_gain)))
    na_bias = _na_bias_table(na_rpb)
    lru_params = (
        lru_conv_w, rows(lru_conv_b),
        bf(0.5 * jnp.concatenate([_block_diag(lru_wa), _block_diag(lru_wx)], axis=-1)),
        0.5 * jnp.concatenate([lru_ba, lru_bx], axis=-1)[:, :, None, :], lru_lambda[:, :, None, :])
    mix_params = (rows(g_mix), rows(g_mlp), w_gate, bf(w_na_o), bf(w_mla_o), bf(w_lru_o), bf(w_o), bf(w_ff1),
                  bf(w_ff2))

    for i in range(depth):
        with_ctx = i < depth - 1
        naq, nak, nav, qm, km, vm, lu, lg = _inproj(i, xs, mod_all, head_ind, rc, rs, proj_params)
        o_na = _na_attention(i, naq, nak, nav, na_bias, with_ctx)
        o_mla = _mla_attention(qm, km, vm, with_ctx)
        o_lru = _lru(i, lu, lg, lru_params)
        xs = _mix_mlp(i, xs, o_na, o_mla, o_lru, mod_all, mix_params, with_ctx)
    return xs
```

```python
import functools

import numpy as np
import jax
import jax.numpy as jnp
from jax import lax
from jax.experimental import pallas as pl
from jax.experimental.pallas import tpu as pltpu

F32 = jnp.float32
BF16 = jnp.bfloat16

D_MODEL = 1024
CTX_LEN = 256
GRID_W = 64
NA_HEADS = 8
NA_DH = 64
NA_KH = 8
NA_KW = 16
NA_WIDTH = NA_HEADS * NA_DH
MLA_HEADS = 8
MLA_NOPE = 64
MLA_ROPE = 32
MLA_V = 64
MLA_QK = MLA_NOPE + MLA_ROPE
MLA_Q_RANK = 384
MLA_KV_RANK = 256
MLA_WIDTH = MLA_HEADS * MLA_V
LRU_W = 512
LRU_BLOCKS = 8
LRU_C = 8.0
CONV_W = 4
D_FF = 4 * D_MODEL
ROPE_BASE = 10000.0
EPS = 1e-6
NEG = -1e30
LOG2E = float(np.log2(np.e))
F32_TINY = float(np.finfo(np.float32).tiny)

LANES = 128
SUBLANES = 8
TM_PROJ = 768
TM_MIX = 576
TM_MIX_LAST = 256
MLA_GROUP = LANES
VMEM_LIMIT = 56 * 1024 * 1024

C_NAQ, C_NAK, C_NAV = 0, 512, 1024
C_MQ = 1536
C_MKV = C_MQ + MLA_Q_RANK
C_MR = C_MKV + MLA_KV_RANK
C_LU = C_MR + LANES
C_LG = C_LU + LRU_W
C_END = C_LG + LRU_W


def _sigmoid(z):
    return 0.5 * jnp.tanh(0.5 * z) + 0.5


def _dot(a, b):
    return jnp.dot(a, b, preferred_element_type=F32)


def _dot_t(a, b):
    return lax.dot_general(a, b, (((1,), (1,)), ((), ())), preferred_element_type=F32)


def _const_spec(shape):
    nd = len(shape)
    return pl.BlockSpec(shape, lambda *_: (0,) * nd, pipeline_mode=pl.Buffered(1))


def _layer_spec(arr, layer):
    nd = arr.ndim
    return pl.BlockSpec((None,) + arr.shape[1:], lambda *_: (layer,) + (0,) * (nd - 1), pipeline_mode=pl.Buffered(1))


def _mod_specs(mod_all, layer, nb):
    shape = (None, 1) + mod_all.shape[2:]
    return [pl.BlockSpec(shape, lambda b, t: (layer, b, 0, 0)), pl.BlockSpec(shape, lambda b, t: (layer, nb, 0, 0))]


def _modulated_norm(x, gain, shift, scale):
    ms = jnp.mean(x * x, axis=-1, keepdims=True)
    return (x * lax.rsqrt(ms + EPS) * gain) * (1.0 + scale) + shift


def _mod_rows(modb_ref, modc_ref, is_ctx):
    return lambda k: jnp.where(is_ctx, modc_ref[0, k:k + 1, :], modb_ref[0, k:k + 1, :])


def _mod_kernel(c_ref, w_ref, b_ref, o_ref):
    cv = c_ref[...]
    s = cv * _sigmoid(cv)
    o_ref[0] = _dot(s.astype(BF16), w_ref[0].astype(BF16)) + b_ref[0]


def _modulation(cvec, w_mod, b_mod):
    depth = w_mod.shape[0]
    rp = cvec.shape[0]
    return pl.pallas_call(
        _mod_kernel,
        grid=(depth, 6),
        in_specs=[
            pl.BlockSpec((rp, D_MODEL), lambda i, j: (0, 0)),
            pl.BlockSpec((1, D_MODEL, D_MODEL), lambda i, j: (i, 0, j)),
            pl.BlockSpec((1, 1, D_MODEL), lambda i, j: (i, 0, j)),
        ],
        out_specs=pl.BlockSpec((1, rp, D_MODEL), lambda i, j: (i, 0, j)),
        out_shape=jax.ShapeDtypeStruct((depth, rp, 6 * D_MODEL), F32),
        name="modulation",
    )(cvec, w_mod, b_mod.reshape(depth, 1, 6 * D_MODEL))


def _head_norm64(p, ind, gain):
    ssq = _dot((p * p).astype(BF16), ind)
    return p * lax.rsqrt(ssq * (1.0 / NA_DH) + EPS) * gain


def _rope(n, rc, rs):
    return n * rc + pltpu.roll(n, LANES // 2, 1) * rs


def _inproj_kernel(x_ref, modb_ref, modc_ref, gmix_ref, w_ref, ind_ref, naqg_ref, nakg_ref, qag_ref, wqb_ref,
                   kvag_ref, wkv_ref, mqg_ref, mkg_ref, rc_ref, rs_ref,
                   naq_ref, nak_ref, nav_ref, qm_ref, km_ref, vm_ref, lu_ref, lg_ref):
    tm = x_ref.shape[1]
    is_ctx = (lax.broadcasted_iota(jnp.int32, (tm, 1), 0) < CTX_LEN) & (pl.program_id(1) == 0)
    mod = _mod_rows(modb_ref, modc_ref, is_ctx)
    h = _modulated_norm(x_ref[0], gmix_ref[...], mod(0), mod(1)).astype(BF16)

    def proj(a, b):
        return _dot(h, w_ref[:, a:b])

    rc, rs = rc_ref[...], rs_ref[...]

    platent = proj(C_MQ, C_LU)
    pq = platent[:, :MLA_Q_RANK]
    pkv = platent[:, MLA_Q_RANK:MLA_Q_RANK + MLA_KV_RANK]
    kr = platent[:, MLA_Q_RANK + MLA_KV_RANK:]

    nq = pq * lax.rsqrt(jnp.mean(pq * pq, axis=-1, keepdims=True) + EPS) * qag_ref[...]
    q0 = _dot(nq.astype(BF16), wqb_ref[...])

    nkv = (pkv * lax.rsqrt(jnp.mean(pkv * pkv, axis=-1, keepdims=True) + EPS) * kvag_ref[...]).astype(BF16)
    kv = _dot(nkv, wkv_ref[...])
    k0 = kv[:, :MLA_HEADS * MLA_GROUP]
    vm_ref[0] = kv[:, MLA_HEADS * MLA_GROUP:].astype(BF16)
    kr_ss = jnp.sum(kr * kr, axis=-1, keepdims=True)
    kr_rot = _rope(kr * mkg_ref[...], rc, rs)

    def q_head(hd):
        sl = slice(MLA_GROUP * hd, MLA_GROUP * (hd + 1))
        blk = q0[:, sl]
        ss = jnp.sum(blk * blk, axis=-1, keepdims=True) * (1.0 / MLA_QK)
        qm_ref[0, :, sl] = _rope(blk * lax.rsqrt(ss + EPS) * mqg_ref[...], rc, rs).astype(BF16)

    def k_head(hd):
        sl = slice(MLA_GROUP * hd, MLA_GROUP * (hd + 1))
        blk = k0[:, sl]
        ss = (jnp.sum(blk * blk, axis=-1, keepdims=True) + kr_ss) * (1.0 / MLA_QK)
        km_ref[0, :, sl] = ((blk * mkg_ref[...] + kr_rot) * lax.rsqrt(ss + EPS)).astype(BF16)

    def wide(i):
        if i == 0:
            naq_ref[0] = _head_norm64(proj(C_NAQ, C_NAK), ind_ref[...], naqg_ref[...]).astype(BF16)
        elif i == 1:
            nak_ref[0] = _head_norm64(proj(C_NAK, C_NAV), ind_ref[...], nakg_ref[...]).astype(BF16)
        elif i == 2:
            nav_ref[0] = proj(C_NAV, C_MQ).astype(BF16)
        elif i == 3:
            lu_ref[0] = proj(C_LU, C_LG).astype(BF16)
        else:
            lg_ref[0] = proj(C_LG, C_END).astype(BF16)

    for i in range(5):
        wide(i)
        for hd in range(2 * i, min(2 * i + 2, MLA_HEADS)):
            q_head(hd)
            k_head(hd)


def _inproj(layer, xs, mod_all, ind, rc, rs, params):
    nb, tt, _ = xs.shape
    tm = TM_PROJ
    tok = lambda width: pl.BlockSpec((1, tm, width), lambda b, t: (b, t, 0))
    rope = pl.BlockSpec((tm, LANES), lambda b, t: (t, 0))
    out_widths = (NA_WIDTH, NA_WIDTH, NA_WIDTH, MLA_HEADS * MLA_GROUP, MLA_HEADS * MLA_GROUP, MLA_WIDTH,
                  LRU_W, LRU_W)
    gmix, w, *rest = params
    return pl.pallas_call(
        _inproj_kernel,
        grid=(nb, tt // tm),
        in_specs=[
            tok(D_MODEL), *_mod_specs(mod_all, layer, nb),
            _layer_spec(gmix, layer), _layer_spec(w, layer), _const_spec(ind.shape),
            *[_layer_spec(a, layer) for a in rest],
            rope, rope,
        ],
        out_specs=[tok(wd) for wd in out_widths],
        out_shape=[jax.ShapeDtypeStruct((nb, tt, wd), BF16) for wd in out_widths],
        compiler_params=pltpu.CompilerParams(vmem_limit_bytes=VMEM_LIMIT),
        name="inproj",
    )(xs, mod_all, mod_all, gmix, w, ind, *rest, rc, rs)


NA_WIN = NA_KH * GRID_W
NA_TQ = 1024
NA_ROWS_PER_STEP = NA_TQ // GRID_W


def _softmax_pv(s, v):
    p = jnp.exp2(s - jnp.max(s, axis=-1, keepdims=True))
    return _dot(p.astype(BF16), v) * (1.0 / jnp.sum(p, axis=-1, keepdims=True))


def _stack_heads(q):
    lo = lax.broadcasted_iota(jnp.int32, q.shape, 1) < NA_DH
    return jnp.concatenate([jnp.where(lo, q, 0), jnp.where(lo, 0, q)], axis=0)


def _unstack_heads(o):
    n = o.shape[0] // 2
    lo = lax.broadcasted_iota(jnp.int32, (n, LANES), 1) < NA_DH
    return jnp.where(lo, o[:n], o[n:])


def _na_kernel(q_ref, k_ref, v_ref, bias_ref, o_ref, *, with_ctx):
    i = pl.program_id(1)
    npair = NA_WIDTH // LANES

    def ctx_block():
        outs = []
        for j in range(npair):
            cols = slice(LANES * j, LANES * (j + 1))
            s_c = _dot_t(_stack_heads(q_ref[0, 0:CTX_LEN, cols]), k_ref[0, 0:CTX_LEN, cols])
            outs.append(_unstack_heads(_softmax_pv(s_c, v_ref[0, 0:CTX_LEN, cols])))
        o_ref[0, 0:CTX_LEN, :] = jnp.concatenate(outs, axis=-1).astype(BF16)

    def latent_rows(blk):
        ng = NA_ROWS_PER_STEP
        qrows = pl.ds(pl.multiple_of(CTX_LEN + blk * NA_TQ, CTX_LEN), NA_TQ)
        sq = 2 * GRID_W
        offs, wins = [], []
        for g in range(ng):
            r = blk * ng + g
            rs = jnp.clip(r - NA_KH // 2, 0, GRID_W // 2 - NA_KH)
            offs.append(rs - r + NA_KH - 1)
            wins.append(pl.ds(pl.multiple_of(CTX_LEN + rs * GRID_W, GRID_W), NA_WIN))
        lo = lax.broadcasted_iota(jnp.int32, (NA_TQ, LANES), 1) < NA_DH
        grp = lambda z, g: z[sq * g:sq * (g + 1)]

        def scores(j):
            cols = slice(LANES * j, LANES * (j + 1))
            q = q_ref[0, qrows, cols]
            q_lo, q_hi = jnp.where(lo, q, 0), jnp.where(lo, 0, q)
            qst = jnp.concatenate([part[GRID_W * g:GRID_W * (g + 1)] for g in range(ng) for part in (q_lo, q_hi)],
                                  axis=0)
            s_c = _dot_t(qst, k_ref[0, 0:CTX_LEN, cols])
            s_w = [_dot_t(grp(qst, g), k_ref[0, wins[g], cols])
                   + bias_ref[offs[g], 2 * j:2 * j + 2].reshape(sq, NA_WIN) for g in range(ng)]
            return s_c, s_w

        def numerators(s_c, s_w):
            pcs, pws, ls = [], [], []
            for g in range(ng):
                sc, sw = grp(s_c, g), s_w[g]
                m = jnp.maximum(jnp.max(sc, axis=-1, keepdims=True), jnp.max(sw, axis=-1, keepdims=True))
                pc, pw = jnp.exp2(sc - m), jnp.exp2(sw - m)
                ls.append(1.0 / (jnp.sum(pc, axis=-1, keepdims=True) + jnp.sum(pw, axis=-1, keepdims=True)))
                pcs.append(pc.astype(BF16))
                pws.append(pw.astype(BF16))
            return jnp.concatenate(pcs, axis=0), pws, ls

        def values(j, p_c, p_w, inv_l):
            cols = slice(LANES * j, LANES * (j + 1))
            o_c = _dot(p_c, v_ref[0, 0:CTX_LEN, cols])
            rows = [_unstack_heads((grp(o_c, g) + _dot(p_w[g], v_ref[0, wins[g], cols])) * inv_l[g])
                    for g in range(ng)]
            return jnp.concatenate(rows, axis=0)

        outs = []
        s_next = scores(0)
        for j in range(npair):
            s_cur = s_next
            if j + 1 < npair:
                s_next = scores(j + 1)
            outs.append(values(j, *numerators(*s_cur)))
        o_ref[0, qrows, :] = jnp.concatenate(outs, axis=-1).astype(BF16)

    if with_ctx:
        pl.when(i == 0)(ctx_block)
        pl.when(i > 0)(lambda: latent_rows(i - 1))
    else:
        latent_rows(i)


def _na_attention(layer, q, k, v, bias, with_ctx):
    nb, tt, _ = q.shape
    full = pl.BlockSpec((1, tt, NA_WIDTH), lambda b, i: (b, 0, 0))
    return pl.pallas_call(
        functools.partial(_na_kernel, with_ctx=with_ctx),
        grid=(nb, (tt - CTX_LEN) // NA_TQ + (1 if with_ctx else 0)),
        in_specs=[full, full, full, _layer_spec(bias, layer)],
        out_specs=full,
        out_shape=jax.ShapeDtypeStruct((nb, tt, NA_WIDTH), BF16),
        compiler_params=pltpu.CompilerParams(vmem_limit_bytes=VMEM_LIMIT),
        name="na_attention",
    )(q, k, v, bias)


def _na_bias_table(rpb):
    qc = np.arange(GRID_W)[:, None]
    kc = np.arange(GRID_W)[None, :]
    cs = np.clip(qc - NA_KW // 2, 0, GRID_W - NA_KW)
    valid = (kc >= cs) & (kc < cs + NA_KW)
    dc = np.clip(kc - qc + NA_KW - 1, 0, 2 * NA_KW - 2)
    onehot = (dc[:, :, None] == np.arange(2 * NA_KW - 1)).astype(np.float32)
    per_row = jnp.einsum('lhdc,qkc->lhdqk', rpb, onehot, precision=lax.Precision.HIGHEST) * LOG2E
    per_row = jnp.where(valid, per_row, NEG)
    tab = jnp.stack([per_row[:, :, off:off + NA_KH] for off in range(NA_KH)], axis=1)
    tab = jnp.transpose(tab, (0, 1, 2, 4, 3, 5))
    return tab.reshape(rpb.shape[0], NA_KH, NA_HEADS, GRID_W, NA_WIN).astype(F32)


MLA_TQ = 512


def _mla_kernel(q_ref, k_ref, v_ref, o_ref, *, with_ctx):
    i = pl.program_id(1)
    tt = k_ref.shape[1]

    def run(row0, nq, nk):
        rows = pl.ds(row0, nq)
        lo = lax.broadcasted_iota(jnp.int32, (nq, LANES), 1) < MLA_V

        def scores(hd):
            hs = slice(MLA_GROUP * hd, MLA_GROUP * (hd + 1))
            return _dot_t(q_ref[0, rows, hs], k_ref[0, 0:nk, hs])

        def values(hd, p, inv_l):
            vcols = slice(LANES * (hd // 2), LANES * (hd // 2 + 1))
            return _dot(p, v_ref[0, 0:nk, vcols]) * inv_l

        outs = []
        s_next = scores(0)
        for hd in range(MLA_HEADS):
            s = s_next
            if hd + 1 < MLA_HEADS:
                s_next = scores(hd + 1)
            p = jnp.exp2(s - jnp.max(s, axis=-1, keepdims=True))
            outs.append(values(hd, p.astype(BF16), 1.0 / jnp.sum(p, axis=-1, keepdims=True)))
        pairs = [jnp.where(lo, outs[2 * j], outs[2 * j + 1]) for j in range(MLA_WIDTH // LANES)]
        o_ref[0, rows, :] = jnp.concatenate(pairs, axis=-1).astype(BF16)

    latent_row0 = lambda blk: pl.multiple_of(CTX_LEN + blk * MLA_TQ, CTX_LEN)
    if with_ctx:
        pl.when(i == 0)(lambda: run(0, CTX_LEN, CTX_LEN))
        pl.when(i > 0)(lambda: run(latent_row0(i - 1), MLA_TQ, tt))
    else:
        run(latent_row0(i), MLA_TQ, tt)


def _mla_attention(q, k, v, with_ctx):
    nb, tt, _ = q.shape
    whole = lambda width: pl.BlockSpec((1, tt, width), lambda b, i: (b, 0, 0))
    return pl.pallas_call(
        functools.partial(_mla_kernel, with_ctx=with_ctx),
        grid=(nb, (tt - CTX_LEN) // MLA_TQ + (1 if with_ctx else 0)),
        in_specs=[whole(MLA_HEADS * MLA_GROUP), whole(MLA_HEADS * MLA_GROUP), whole(MLA_WIDTH)],
        out_specs=whole(MLA_WIDTH),
        out_shape=jax.ShapeDtypeStruct((nb, tt, MLA_WIDTH), BF16),
        compiler_params=pltpu.CompilerParams(vmem_limit_bytes=VMEM_LIMIT),
        name="mla_attention",
    )(q, k, v)


LRU_CHUNK = 256
LRU_TILES = LRU_CHUNK // SUBLANES


def _chunk_permutation():
    i = np.arange(LRU_CHUNK)
    p = np.zeros((LRU_CHUNK, LRU_CHUNK), np.float32)
    p[i, LRU_TILES * (i % SUBLANES) + i // SUBLANES] = 1.0
    return p


def _scan_tiles(a, b, carry, reverse):
    order = list(reversed(range(LRU_TILES))) if reverse else list(range(LRU_TILES))
    acum, bcum = [None] * LRU_TILES, [None] * LRU_TILES
    prev = None
    for k in order:
        if prev is None:
            acum[k], bcum[k] = a[k], b[k]
        else:
            acum[k] = a[k] * acum[prev]
            bcum[k] = a[k] * bcum[prev] + b[k]
        prev = k
    p_end, e_end = acum[prev], bcum[prev]
    sub = list(reversed(range(SUBLANES))) if reverse else list(range(SUBLANES))
    cin = [None] * SUBLANES
    for s in sub:
        cin[s] = carry
        carry = p_end[s:s + 1, :] * carry + e_end[s:s + 1, :]
    cin = jnp.concatenate(cin, axis=0)
    return [bcum[k] + acum[k] * cin for k in range(LRU_TILES)], carry


def _lru_kernel(lu_ref, lg_ref, perm_ref, permt_ref, cw_ref, cb_ref, wg_ref, bg_ref, lam_ref, o_ref,
                lp_ref, u_ref, hs_ref, hb_ref):
    tt = lu_ref.shape[1]
    nchunk = tt // LRU_CHUNK
    chunk_rows = lambda c: slice(c * LRU_CHUNK, (c + 1) * LRU_CHUNK)
    for c in range(nchunk):
        lp_ref[chunk_rows(c), :] = _dot(perm_ref[...], lu_ref[0, chunk_rows(c), :])

    left = CONV_W // 2
    sub = lax.broadcasted_iota(jnp.int32, (SUBLANES, LRU_W), 0)

    def conv_chunk(c, _):
        base = pl.multiple_of(c * LRU_CHUNK, LRU_CHUNK)
        tile = lambda b, k: lp_ref[pl.ds(b + SUBLANES * k, SUBLANES), :]
        prev_base = pl.multiple_of(jnp.maximum(c - 1, 0) * LRU_CHUNK, LRU_CHUNK)
        next_base = pl.multiple_of(jnp.minimum(c + 1, nchunk - 1) * LRU_CHUNK, LRU_CHUNK)
        has_prev = c >= 2
        has_next = (c >= 1) & (c < nchunk - 1)

        def earlier(k):
            edge = jnp.where(has_prev, tile(prev_base, k), 0.0)
            return pltpu.roll(jnp.where(sub == SUBLANES - 1, edge, tile(base, k)), 1, 0)

        def later(k):
            edge = jnp.where(has_next, tile(next_base, k), 0.0)
            return pltpu.roll(jnp.where(sub == 0, edge, tile(base, k)), SUBLANES - 1, 0)

        for k in range(LRU_TILES):
            acc = jnp.broadcast_to(cb_ref[...], (SUBLANES, LRU_W))
            for j in range(CONV_W):
                kk = k + j - left
                if kk < 0:
                    tap = earlier(kk + LRU_TILES)
                elif kk >= LRU_TILES:
                    tap = later(kk - LRU_TILES)
                else:
                    tap = tile(base, kk)
                acc = acc + tap * cw_ref[j:j + 1, :]
            u_ref[pl.ds(base + SUBLANES * k, SUBLANES), :] = acc
        return 0

    lax.fori_loop(0, nchunk, conv_chunk, 0)

    def half_decay_scale(d):
        nlam = -lam_ref[d]
        sp = jnp.maximum(nlam, 0.0) + jnp.log(1.0 + jnp.exp(-jnp.abs(nlam)))
        return (0.5 * LRU_C) * sp

    half_c = (half_decay_scale(0), half_decay_scale(1))

    def chunk_scan(d, c, carry, h_ref):
        reverse = d == 1
        rows = pl.ds(pl.multiple_of(c * LRU_CHUNK, LRU_CHUNK), LRU_CHUNK)
        uh = u_ref[rows, :]
        th = jnp.tanh(_dot(uh.astype(BF16), wg_ref[d]) + bg_ref[d])
        neg_log_a = half_c[d] * th[:, :LRU_W] + half_c[d]
        a = jnp.exp2(neg_log_a * (-LOG2E))
        x = jnp.tanh(neg_log_a) * (a * a + 1.0)
        root = x * lax.rsqrt(jnp.maximum(x, F32_TINY))
        bterm = root * (th[:, LRU_W:] + 1.0) * uh
        split = lambda z: [z[SUBLANES * k:SUBLANES * (k + 1), :] for k in range(LRU_TILES)]
        h, carry = _scan_tiles(split(a), split(bterm), carry, reverse)
        h_ref[rows, :] = jnp.concatenate(h, axis=0)
        return carry

    def both_directions(i, carries):
        fwd = chunk_scan(0, i, carries[0], hs_ref)
        bwd = chunk_scan(1, jnp.where(i == 0, 0, nchunk - i), carries[1], hb_ref)
        return fwd, bwd

    zero = jnp.zeros((1, LRU_W), F32)
    lax.fori_loop(0, nchunk, both_directions, (zero, zero))

    k0 = float(np.sqrt(2.0 / np.pi))
    def gate_tile_order(c):
        return _dot(perm_ref[...], lg_ref[0, chunk_rows(c), :])

    z_next = gate_tile_order(0)
    for c in range(nchunk):
        z = z_next
        if c + 1 < nchunk:
            z_next = gate_tile_order(c + 1)
        gelu = z * (0.5 * (1.0 + jnp.tanh(k0 * (z + 0.044715 * (z * z * z)))))
        o_tile_order = (gelu * (hs_ref[chunk_rows(c), :] + hb_ref[chunk_rows(c), :])).astype(BF16)
        o_ref[0, chunk_rows(c), :] = _dot(permt_ref[...], o_tile_order).astype(BF16)


def _lru(layer, lu, lg, params):
    nb, tt, _ = lu.shape
    full = pl.BlockSpec((1, tt, LRU_W), lambda b: (b, 0, 0))
    perm = _chunk_permutation()
    perms = (jnp.asarray(perm, BF16), jnp.asarray(perm.T, BF16))
    consts = perms + tuple(params)
    return pl.pallas_call(
        _lru_kernel,
        grid=(nb,),
        in_specs=[full, full, *[_const_spec(a.shape) for a in perms], *[_layer_spec(a, layer) for a in params]],
        out_specs=full,
        out_shape=jax.ShapeDtypeStruct((nb, tt, LRU_W), BF16),
        scratch_shapes=[pltpu.VMEM((tt, LRU_W), F32)] * 4,
        compiler_params=pltpu.CompilerParams(vmem_limit_bytes=VMEM_LIMIT),
        name="rglru",
    )(lu, lg, *consts)


FF_CHUNK = 2048


def _mix_mlp_kernel(x_ref, ona_ref, omla_ref, olru_ref, modb_ref, modc_ref, gmix_ref, gmlp_ref, wgt_ref, wna_ref,
                    wmla_ref, wlru_ref, wo_ref, w1_ref, w2_ref, o_ref, *, t_off):
    tm = x_ref.shape[1]
    is_ctx = (lax.broadcasted_iota(jnp.int32, (tm, 1), 0) < CTX_LEN) & (pl.program_id(1) + t_off == 0)
    mod = _mod_rows(modb_ref, modc_ref, is_ctx)
    x = x_ref[0]
    h = _modulated_norm(x, gmix_ref[...], mod(0), mod(1)).astype(BF16)
    gates = _sigmoid(_dot(h, wgt_ref[...]))
    y = None
    for g, (o_r, w_r) in enumerate(((ona_ref, wna_ref), (omla_ref, wmla_ref), (olru_ref, wlru_ref))):
        term = gates[:, D_MODEL * g:D_MODEL * (g + 1)] * _dot(o_r[0], w_r[...])
        y = term if y is None else y + term
    x1 = x + mod(2) * _dot(y.astype(BF16), wo_ref[...])

    h2 = _modulated_norm(x1, gmlp_ref[...], mod(3), mod(4)).astype(BF16)
    acc = jnp.zeros_like(x1)
    for c in range(D_FF // FF_CHUNK):
        a = jnp.maximum(_dot(h2, w1_ref[:, FF_CHUNK * c:FF_CHUNK * (c + 1)]), 0.0)
        acc = acc + _dot((a * a).astype(BF16), w2_ref[FF_CHUNK * c:FF_CHUNK * (c + 1), :])
    o_ref[0] = x1 + mod(5) * acc


def _mix_mlp(layer, xs, ona, omla, olru, mod_all, params, with_ctx):
    nb, tt, _ = xs.shape
    tm = TM_MIX if with_ctx else TM_MIX_LAST
    t_off = 0 if with_ctx else CTX_LEN // tm
    nt = tt // tm - t_off
    tok = lambda width: pl.BlockSpec((1, tm, width), lambda b, i: (b, i + t_off, 0))
    return pl.pallas_call(
        functools.partial(_mix_mlp_kernel, t_off=t_off),
        grid=(nb, nt),
        in_specs=[
            tok(D_MODEL), tok(NA_WIDTH), tok(MLA_WIDTH), tok(LRU_W), *_mod_specs(mod_all, layer, nb),
            *[_layer_spec(a, layer) for a in params],
        ],
        out_specs=pl.BlockSpec((1, tm, D_MODEL), lambda b, i: (b, i, 0)),
        out_shape=jax.ShapeDtypeStruct((nb, nt * tm, D_MODEL), F32),
        compiler_params=pltpu.CompilerParams(vmem_limit_bytes=VMEM_LIMIT),
        name="mix_mlp",
    )(xs, ona, omla, olru, mod_all, mod_all, *params)


def _group_source():
    nf = MLA_ROPE // 4
    half = LANES // 2
    src = np.full(LANES, -1)
    for part in (0, 1):
        lane0 = half * part
        src[lane0:lane0 + MLA_NOPE // 2] = MLA_NOPE // 2 * part + np.arange(MLA_NOPE // 2)
        src[lane0 + MLA_NOPE // 2:lane0 + MLA_NOPE // 2 + nf] = MLA_NOPE + nf * part + np.arange(nf)
        src[lane0 + MLA_NOPE // 2 + nf:lane0 + MLA_NOPE // 2 + 2 * nf] = MLA_NOPE + 2 * nf + nf * part + np.arange(nf)
    return src


_GROUP_SRC = _group_source()


def _to_group(v):
    return jnp.where(_GROUP_SRC >= 0, jnp.take(v, np.maximum(_GROUP_SRC, 0), axis=-1), 0)


def _pad_last(v, before, after):
    return jnp.pad(v, [(0, 0)] * (v.ndim - 1) + [(before, after)])


def _heads_to_groups(w, width):
    return _to_group(_pad_last(w, 0, MLA_QK - width)).reshape(*w.shape[:-2], MLA_HEADS * MLA_GROUP)


def _relayout_w_in(w):
    o_mr = 3 * NA_WIDTH + MLA_Q_RANK + MLA_KV_RANK
    o_gt = o_mr + MLA_ROPE + 2 * LRU_W
    mr_group = _to_group(_pad_last(w[..., o_mr:o_mr + MLA_ROPE], MLA_NOPE, 0))
    w_proj = jnp.concatenate([w[..., :o_mr], mr_group, w[..., o_mr + MLA_ROPE:o_gt]], axis=-1)
    return w_proj.astype(BF16), w[..., o_gt:].astype(BF16)


def _rope_tables(tt):
    p = np.arange(tt - CTX_LEN)
    nf = MLA_ROPE // 4
    inv = ROPE_BASE ** (-jnp.arange(nf, dtype=F32) / nf)
    cos, sin = [], []
    for pos in (jnp.asarray(p // GRID_W, F32), jnp.asarray(p % GRID_W, F32)):
        ang = pos[:, None] * inv
        cos += [jnp.cos(ang), jnp.cos(ang)]
        sin += [-jnp.sin(ang), jnp.sin(ang)]
    ones = jnp.ones((tt - CTX_LEN, MLA_NOPE), F32)
    rc = jnp.concatenate([jnp.ones((CTX_LEN, MLA_QK), F32), jnp.concatenate([ones] + cos, axis=1)], axis=0)
    rs = jnp.concatenate([jnp.zeros((CTX_LEN, MLA_QK), F32), jnp.concatenate([0.0 * ones] + sin, axis=1)], axis=0)
    return _to_group(rc), _to_group(rs)


def _block_diag(w):
    eye = jnp.eye(LRU_BLOCKS, dtype=w.dtype)
    return jnp.einsum('...ncd,nm->...ncmd', w, eye).reshape(*w.shape[:-3], LRU_W, LRU_W)


def kernel(x, c, ctx, c_ctx, w_mod, b_mod, g_mix, g_mlp, w_in, na_q_gain, na_k_gain, na_rpb, mla_qa_gain, w_q_b,
           mla_kva_gain, w_kv_b, mla_q_gain, mla_k_gain, lru_conv_w, lru_conv_b, lru_wa, lru_ba, lru_wx, lru_bx,
           lru_lambda, w_na_o, w_mla_o, w_lru_o, w_o, w_ff1, w_ff2):
    nb, seq, _ = x.shape
    depth = w_in.shape[0]
    assert ctx.shape[1] == CTX_LEN and seq // GRID_W == GRID_W // 2 and seq % GRID_W == 0
    tt = CTX_LEN + seq
    assert tt % TM_PROJ == 0 and tt % TM_MIX == 0 and tt % LRU_CHUNK == 0 and CTX_LEN % TM_MIX_LAST == 0
    assert seq % NA_TQ == 0 and seq % MLA_TQ == 0 and seq % TM_MIX_LAST == 0

    xs = jnp.concatenate([ctx, x], axis=1)
    rp = -(-(nb + 1) // SUBLANES) * SUBLANES
    cvec = jnp.concatenate([c, c_ctx[None, :], jnp.zeros((rp - nb - 1, D_MODEL), F32)], axis=0)
    mod_all = _modulation(cvec, w_mod, b_mod).reshape(depth, rp, 6, D_MODEL)
    rc, rs = _rope_tables(tt)
    head_ind = jnp.asarray(np.kron(np.eye(NA_HEADS), np.ones((NA_DH, NA_DH))), BF16)

    rows = lambda v: v[:, None, :]
    bf = lambda v: v.astype(BF16)
    kvb = w_kv_b.reshape(depth, MLA_KV_RANK, MLA_HEADS, MLA_NOPE + MLA_V)
    w_proj, w_gate = _relayout_w_in(w_in)
    proj_params = (
        rows(g_mix), w_proj,
        rows(jnp.tile(na_q_gain, (1, NA_HEADS)) * (NA_DH ** -0.5 * LOG2E)), rows(jnp.tile(na_k_gain, (1, NA_HEADS))),
        rows(mla_qa_gain), bf(_heads_to_groups(w_q_b.reshape(depth, MLA_Q_RANK, MLA_HEADS, MLA_QK), MLA_QK)),
        rows(mla_kva_gain),
        bf(jnp.concatenate([_heads_to_groups(kvb[..., :MLA_NOPE], MLA_NOPE),
                            kvb[..., MLA_NOPE:].reshape(depth, MLA_KV_RANK, MLA_WIDTH)], axis=-1)),
        rows(_to_group(mla_q_gain) * (MLA_QK ** -0.5 * LOG2E)), rows(_to_group(mla_k_gain)))
    na_bias = _na_bias_table(na_rpb)
    lru_params = (
        0.5 * lru_conv_w, rows(0.5 * lru_conv_b),
        bf(jnp.concatenate([_block_diag(lru_wa), _block_diag(lru_wx)], axis=-1)),
        0.5 * jnp.concatenate([lru_ba, lru_bx], axis=-1)[:, :, None, :], lru_lambda[:, :, None, :])
    mix_params = (rows(g_mix), rows(g_mlp), w_gate, bf(w_na_o), bf(w_mla_o), bf(w_lru_o), bf(w_o), bf(w_ff1),
                  bf(w_ff2))

    for i in range(depth):
        with_ctx = i < depth - 1
        naq, nak, nav, qm, km, vm, lu, lg = _inproj(i, xs, mod_all, head_ind, rc, rs, proj_params)
        o_na = _na_attention(i, naq, nak, nav, na_bias, with_ctx)
        o_mla = _mla_attention(qm, km, vm, with_ctx)
        o_lru = _lru(i, lu, lg, lru_params)
        xs = _mix_mlp(i, xs, o_na, o_mla, o_lru, mod_all, mix_params, with_ctx)
    return xs
```

```python
import functools

import numpy as np
import jax
import jax.numpy as jnp
from jax import lax
from jax.experimental import pallas as pl
from jax.experimental.pallas import tpu as pltpu

F32 = jnp.float32
BF16 = jnp.bfloat16

D_MODEL = 1024
CTX_LEN = 256
GRID_W = 64
NA_HEADS = 8
NA_DH = 64
NA_KH = 8
NA_KW = 16
NA_WIDTH = NA_HEADS * NA_DH
MLA_HEADS = 8
MLA_NOPE = 64
MLA_ROPE = 32
MLA_V = 64
MLA_QK = MLA_NOPE + MLA_ROPE
MLA_Q_RANK = 384
MLA_KV_RANK = 256
MLA_WIDTH = MLA_HEADS * MLA_V
LRU_W = 512
LRU_BLOCKS = 8
LRU_C = 8.0
CONV_W = 4
D_FF = 4 * D_MODEL
ROPE_BASE = 10000.0
EPS = 1e-6
NEG = -1e30
LOG2E = float(np.log2(np.e))
F32_TINY = float(np.finfo(np.float32).tiny)

LANES = 128
SUBLANES = 8
TM_PROJ = 768
TM_MIX = 768
MLA_GROUP = LANES
VMEM_LIMIT = 56 * 1024 * 1024

C_NAQ, C_NAK, C_NAV = 0, 512, 1024
C_MQ = 1536
C_MKV = C_MQ + MLA_Q_RANK
C_MR = C_MKV + MLA_KV_RANK
C_LU = C_MR + LANES
C_LG = C_LU + LRU_W
C_END = C_LG + LRU_W


def _sigmoid(z):
    return 0.5 * jnp.tanh(0.5 * z) + 0.5


def _dot(a, b):
    return jnp.dot(a, b, preferred_element_type=F32)


def _dot_t(a, b):
    return lax.dot_general(a, b, (((1,), (1,)), ((), ())), preferred_element_type=F32)


def _const_spec(shape):
    nd = len(shape)
    return pl.BlockSpec(shape, lambda *_: (0,) * nd, pipeline_mode=pl.Buffered(1))


def _layer_spec(arr, layer):
    nd = arr.ndim
    return pl.BlockSpec((None,) + arr.shape[1:], lambda *_: (layer,) + (0,) * (nd - 1), pipeline_mode=pl.Buffered(1))


def _mod_specs(mod_all, layer, nb):
    shape = (None, 1) + mod_all.shape[2:]
    return [pl.BlockSpec(shape, lambda b, t: (layer, b, 0, 0)), pl.BlockSpec(shape, lambda b, t: (layer, nb, 0, 0))]


def _modulated_norm(x, gain, shift, scale):
    ms = jnp.mean(x * x, axis=-1, keepdims=True)
    return (x * lax.rsqrt(ms + EPS) * gain) * (1.0 + scale) + shift


def _mod_rows(modb_ref, modc_ref, is_ctx):
    return lambda k: jnp.where(is_ctx, modc_ref[0, k:k + 1, :], modb_ref[0, k:k + 1, :])


def _mod_kernel(c_ref, w_ref, b_ref, o_ref):
    cv = c_ref[...]
    s = cv * _sigmoid(cv)
    o_ref[0] = _dot(s.astype(BF16), w_ref[0].astype(BF16)) + b_ref[0]


def _modulation(cvec, w_mod, b_mod):
    depth = w_mod.shape[0]
    rp = cvec.shape[0]
    return pl.pallas_call(
        _mod_kernel,
        grid=(depth, 6),
        in_specs=[
            pl.BlockSpec((rp, D_MODEL), lambda i, j: (0, 0)),
            pl.BlockSpec((1, D_MODEL, D_MODEL), lambda i, j: (i, 0, j)),
            pl.BlockSpec((1, 1, D_MODEL), lambda i, j: (i, 0, j)),
        ],
        out_specs=pl.BlockSpec((1, rp, D_MODEL), lambda i, j: (i, 0, j)),
        out_shape=jax.ShapeDtypeStruct((depth, rp, 6 * D_MODEL), F32),
        name="modulation",
    )(cvec, w_mod, b_mod.reshape(depth, 1, 6 * D_MODEL))


def _head_norm64(p, ind, gain):
    ssq = _dot((p * p).astype(BF16), ind)
    return p * lax.rsqrt(ssq * (1.0 / NA_DH) + EPS) * gain


def _rope(n, rc, rs):
    return n * rc + pltpu.roll(n, LANES // 2, 1) * rs


def _inproj_kernel(x_ref, modb_ref, modc_ref, gmix_ref, w_ref, ind_ref, naqg_ref, nakg_ref, qag_ref, wqb_ref,
                   kvag_ref, wkv_ref, mqg_ref, mkg_ref, rc_ref, rs_ref,
                   naq_ref, nak_ref, nav_ref, qm_ref, km_ref, vm_ref, lu_ref, lg_ref):
    tm = x_ref.shape[1]
    is_ctx = (lax.broadcasted_iota(jnp.int32, (tm, 1), 0) < CTX_LEN) & (pl.program_id(1) == 0)
    mod = _mod_rows(modb_ref, modc_ref, is_ctx)
    h = _modulated_norm(x_ref[0], gmix_ref[...], mod(0), mod(1)).astype(BF16)

    def proj(a, b):
        return _dot(h, w_ref[:, a:b])

    rc, rs = rc_ref[...], rs_ref[...]

    platent = proj(C_MQ, C_LU)
    pq = platent[:, :MLA_Q_RANK]
    pkv = platent[:, MLA_Q_RANK:MLA_Q_RANK + MLA_KV_RANK]
    kr = platent[:, MLA_Q_RANK + MLA_KV_RANK:]

    nq = pq * lax.rsqrt(jnp.mean(pq * pq, axis=-1, keepdims=True) + EPS) * qag_ref[...]
    q0 = _dot(nq.astype(BF16), wqb_ref[...])

    nkv = (pkv * lax.rsqrt(jnp.mean(pkv * pkv, axis=-1, keepdims=True) + EPS) * kvag_ref[...]).astype(BF16)
    kv = _dot(nkv, wkv_ref[...])
    k0 = kv[:, :MLA_HEADS * MLA_GROUP]
    vm_ref[0] = kv[:, MLA_HEADS * MLA_GROUP:].astype(BF16)
    kr_ss = jnp.sum(kr * kr, axis=-1, keepdims=True)
    kr_rot = _rope(kr * mkg_ref[...], rc, rs)

    def q_head(hd):
        sl = slice(MLA_GROUP * hd, MLA_GROUP * (hd + 1))
        blk = q0[:, sl]
        ss = jnp.sum(blk * blk, axis=-1, keepdims=True) * (1.0 / MLA_QK)
        qm_ref[0, :, sl] = _rope(blk * lax.rsqrt(ss + EPS) * mqg_ref[...], rc, rs).astype(BF16)

    def k_head(hd):
        sl = slice(MLA_GROUP * hd, MLA_GROUP * (hd + 1))
        blk = k0[:, sl]
        ss = (jnp.sum(blk * blk, axis=-1, keepdims=True) + kr_ss) * (1.0 / MLA_QK)
        km_ref[0, :, sl] = ((blk * mkg_ref[...] + kr_rot) * lax.rsqrt(ss + EPS)).astype(BF16)

    def wide(i):
        if i == 0:
            naq_ref[0] = _head_norm64(proj(C_NAQ, C_NAK), ind_ref[...], naqg_ref[...]).astype(BF16)
        elif i == 1:
            nak_ref[0] = _head_norm64(proj(C_NAK, C_NAV), ind_ref[...], nakg_ref[...]).astype(BF16)
        elif i == 2:
            nav_ref[0] = proj(C_NAV, C_MQ).astype(BF16)
        elif i == 3:
            lu_ref[0] = proj(C_LU, C_LG).astype(BF16)
        else:
            lg_ref[0] = proj(C_LG, C_END).astype(BF16)

    for i in range(5):
        wide(i)
        for hd in range(2 * i, min(2 * i + 2, MLA_HEADS)):
            q_head(hd)
            k_head(hd)


def _inproj(layer, xs, mod_all, ind, rc, rs, params):
    nb, tt, _ = xs.shape
    tm = TM_PROJ
    tok = lambda width: pl.BlockSpec((1, tm, width), lambda b, t: (b, t, 0))
    rope = pl.BlockSpec((tm, LANES), lambda b, t: (t, 0))
    out_widths = (NA_WIDTH, NA_WIDTH, NA_WIDTH, MLA_HEADS * MLA_GROUP, MLA_HEADS * MLA_GROUP, MLA_WIDTH,
                  LRU_W, LRU_W)
    gmix, w, *rest = params
    return pl.pallas_call(
        _inproj_kernel,
        grid=(nb, tt // tm),
        in_specs=[
            tok(D_MODEL), *_mod_specs(mod_all, layer, nb),
            _layer_spec(gmix, layer), _layer_spec(w, layer), _const_spec(ind.shape),
            *[_layer_spec(a, layer) for a in rest],
            rope, rope,
        ],
        out_specs=[tok(wd) for wd in out_widths],
        out_shape=[jax.ShapeDtypeStruct((nb, tt, wd), BF16) for wd in out_widths],
        compiler_params=pltpu.CompilerParams(vmem_limit_bytes=VMEM_LIMIT),
        name="inproj",
    )(xs, mod_all, mod_all, gmix, w, ind, *rest, rc, rs)


NA_WIN = NA_KH * GRID_W
NA_TQ = 1024
NA_ROWS_PER_STEP = NA_TQ // GRID_W


def _softmax_pv(s, v):
    p = jnp.exp2(s - jnp.max(s, axis=-1, keepdims=True))
    return _dot(p.astype(BF16), v) * (1.0 / jnp.sum(p, axis=-1, keepdims=True))


def _stack_heads(q):
    lo = lax.broadcasted_iota(jnp.int32, q.shape, 1) < NA_DH
    return jnp.concatenate([jnp.where(lo, q, 0), jnp.where(lo, 0, q)], axis=0)


def _unstack_heads(o):
    n = o.shape[0] // 2
    lo = lax.broadcasted_iota(jnp.int32, (n, LANES), 1) < NA_DH
    return jnp.where(lo, o[:n], o[n:])


def _na_kernel(q_ref, k_ref, v_ref, bias_ref, o_ref, *, with_ctx):
    i = pl.program_id(1)
    npair = NA_WIDTH // LANES

    def ctx_block():
        outs = []
        for j in range(npair):
            cols = slice(LANES * j, LANES * (j + 1))
            s_c = _dot_t(_stack_heads(q_ref[0, 0:CTX_LEN, cols]), k_ref[0, 0:CTX_LEN, cols])
            outs.append(_unstack_heads(_softmax_pv(s_c, v_ref[0, 0:CTX_LEN, cols])))
        o_ref[0, 0:CTX_LEN, :] = jnp.concatenate(outs, axis=-1).astype(BF16)

    def latent_rows(blk):
        ng = NA_ROWS_PER_STEP
        qrows = pl.ds(pl.multiple_of(CTX_LEN + blk * NA_TQ, CTX_LEN), NA_TQ)
        sq = 2 * GRID_W
        offs, wins = [], []
        for g in range(ng):
            r = blk * ng + g
            rs = jnp.clip(r - NA_KH // 2, 0, GRID_W // 2 - NA_KH)
            offs.append(rs - r + NA_KH - 1)
            wins.append(pl.ds(pl.multiple_of(CTX_LEN + rs * GRID_W, GRID_W), NA_WIN))
        lo = lax.broadcasted_iota(jnp.int32, (NA_TQ, LANES), 1) < NA_DH
        grp = lambda z, g: z[sq * g:sq * (g + 1)]

        def scores(j):
            cols = slice(LANES * j, LANES * (j + 1))
            q = q_ref[0, qrows, cols]
            q_lo, q_hi = jnp.where(lo, q, 0), jnp.where(lo, 0, q)
            qst = jnp.concatenate([part[GRID_W * g:GRID_W * (g + 1)] for g in range(ng) for part in (q_lo, q_hi)],
                                  axis=0)
            s_c = _dot_t(qst, k_ref[0, 0:CTX_LEN, cols])
            s_w = [_dot_t(grp(qst, g), k_ref[0, wins[g], cols])
                   + bias_ref[offs[g], 2 * j:2 * j + 2].reshape(sq, NA_WIN) for g in range(ng)]
            return s_c, s_w

        def numerators(s_c, s_w):
            pcs, pws, ls = [], [], []
            for g in range(ng):
                sc, sw = grp(s_c, g), s_w[g]
                m = jnp.maximum(jnp.max(sc, axis=-1, keepdims=True), jnp.max(sw, axis=-1, keepdims=True))
                pc, pw = jnp.exp2(sc - m), jnp.exp2(sw - m)
                ls.append(1.0 / (jnp.sum(pc, axis=-1, keepdims=True) + jnp.sum(pw, axis=-1, keepdims=True)))
                pcs.append(pc.astype(BF16))
                pws.append(pw.astype(BF16))
            return jnp.concatenate(pcs, axis=0), pws, ls

        def values(j, p_c, p_w, inv_l):
            cols = slice(LANES * j, LANES * (j + 1))
            o_c = _dot(p_c, v_ref[0, 0:CTX_LEN, cols])
            rows = [_unstack_heads((grp(o_c, g) + _dot(p_w[g], v_ref[0, wins[g], cols])) * inv_l[g])
                    for g in range(ng)]
            return jnp.concatenate(rows, axis=0)

        outs = []
        s_next = scores(0)
        for j in range(npair):
            s_cur = s_next
            if j + 1 < npair:
                s_next = scores(j + 1)
            outs.append(values(j, *numerators(*s_cur)))
        o_ref[0, qrows, :] = jnp.concatenate(outs, axis=-1).astype(BF16)

    if with_ctx:
        pl.when(i == 0)(ctx_block)
        pl.when(i > 0)(lambda: latent_rows(i - 1))
    else:
        @pl.when(i == 0)
        def _():
            o_ref[0, 0:CTX_LEN, :] = jnp.zeros((CTX_LEN, NA_WIDTH), BF16)
        latent_rows(i)


def _na_attention(layer, q, k, v, bias, with_ctx):
    nb, tt, _ = q.shape
    full = pl.BlockSpec((1, tt, NA_WIDTH), lambda b, i: (b, 0, 0))
    return pl.pallas_call(
        functools.partial(_na_kernel, with_ctx=with_ctx),
        grid=(nb, (tt - CTX_LEN) // NA_TQ + (1 if with_ctx else 0)),
        in_specs=[full, full, full, _layer_spec(bias, layer)],
        out_specs=full,
        out_shape=jax.ShapeDtypeStruct((nb, tt, NA_WIDTH), BF16),
        compiler_params=pltpu.CompilerParams(vmem_limit_bytes=VMEM_LIMIT),
        name="na_attention",
    )(q, k, v, bias)


def _na_bias_table(rpb):
    qc = np.arange(GRID_W)[:, None]
    kc = np.arange(GRID_W)[None, :]
    cs = np.clip(qc - NA_KW // 2, 0, GRID_W - NA_KW)
    valid = (kc >= cs) & (kc < cs + NA_KW)
    dc = np.clip(kc - qc + NA_KW - 1, 0, 2 * NA_KW - 2)
    onehot = (dc[:, :, None] == np.arange(2 * NA_KW - 1)).astype(np.float32)
    per_row = jnp.einsum('lhdc,qkc->lhdqk', rpb, onehot, precision=lax.Precision.HIGHEST) * LOG2E
    per_row = jnp.where(valid, per_row, NEG)
    tab = jnp.stack([per_row[:, :, off:off + NA_KH] for off in range(NA_KH)], axis=1)
    tab = jnp.transpose(tab, (0, 1, 2, 4, 3, 5))
    return tab.reshape(rpb.shape[0], NA_KH, NA_HEADS, GRID_W, NA_WIN).astype(F32)


MLA_TQ = 512


def _mla_kernel(q_ref, k_ref, v_ref, o_ref, *, with_ctx):
    i = pl.program_id(1)
    tt = k_ref.shape[1]

    def run(row0, nq, nk):
        rows = pl.ds(row0, nq)
        lo = lax.broadcasted_iota(jnp.int32, (nq, LANES), 1) < MLA_V

        def scores(hd):
            hs = slice(MLA_GROUP * hd, MLA_GROUP * (hd + 1))
            return _dot_t(q_ref[0, rows, hs], k_ref[0, 0:nk, hs])

        def values(hd, p, inv_l):
            vcols = slice(LANES * (hd // 2), LANES * (hd // 2 + 1))
            return _dot(p, v_ref[0, 0:nk, vcols]) * inv_l

        outs = []
        s_next = scores(0)
        for hd in range(MLA_HEADS):
            s = s_next
            if hd + 1 < MLA_HEADS:
                s_next = scores(hd + 1)
            p = jnp.exp2(s - jnp.max(s, axis=-1, keepdims=True))
            outs.append(values(hd, p.astype(BF16), 1.0 / jnp.sum(p, axis=-1, keepdims=True)))
        pairs = [jnp.where(lo, outs[2 * j], outs[2 * j + 1]) for j in range(MLA_WIDTH // LANES)]
        o_ref[0, rows, :] = jnp.concatenate(pairs, axis=-1).astype(BF16)

    latent_row0 = lambda blk: pl.multiple_of(CTX_LEN + blk * MLA_TQ, CTX_LEN)
    if with_ctx:
        pl.when(i == 0)(lambda: run(0, CTX_LEN, CTX_LEN))
        pl.when(i > 0)(lambda: run(latent_row0(i - 1), MLA_TQ, tt))
    else:
        @pl.when(i == 0)
        def _():
            o_ref[0, 0:CTX_LEN, :] = jnp.zeros((CTX_LEN, MLA_WIDTH), BF16)
        run(latent_row0(i), MLA_TQ, tt)


def _mla_attention(q, k, v, with_ctx):
    nb, tt, _ = q.shape
    whole = lambda width: pl.BlockSpec((1, tt, width), lambda b, i: (b, 0, 0))
    return pl.pallas_call(
        functools.partial(_mla_kernel, with_ctx=with_ctx),
        grid=(nb, (tt - CTX_LEN) // MLA_TQ + (1 if with_ctx else 0)),
        in_specs=[whole(MLA_HEADS * MLA_GROUP), whole(MLA_HEADS * MLA_GROUP), whole(MLA_WIDTH)],
        out_specs=whole(MLA_WIDTH),
        out_shape=jax.ShapeDtypeStruct((nb, tt, MLA_WIDTH), BF16),
        compiler_params=pltpu.CompilerParams(vmem_limit_bytes=VMEM_LIMIT),
        name="mla_attention",
    )(q, k, v)


LRU_CHUNK = 256
LRU_TILES = LRU_CHUNK // SUBLANES


def _chunk_permutation():
    i = np.arange(LRU_CHUNK)
    p = np.zeros((LRU_CHUNK, LRU_CHUNK), np.float32)
    p[i, LRU_TILES * (i % SUBLANES) + i // SUBLANES] = 1.0
    return p


def _scan_tiles(a, b, carry, reverse):
    order = list(reversed(range(LRU_TILES))) if reverse else list(range(LRU_TILES))
    acum, bcum = [None] * LRU_TILES, [None] * LRU_TILES
    prev = None
    for k in order:
        if prev is None:
            acum[k], bcum[k] = a[k], b[k]
        else:
            acum[k] = a[k] * acum[prev]
            bcum[k] = a[k] * bcum[prev] + b[k]
        prev = k
    p_end, e_end = acum[prev], bcum[prev]
    sub = list(reversed(range(SUBLANES))) if reverse else list(range(SUBLANES))
    cin = [None] * SUBLANES
    for s in sub:
        cin[s] = carry
        carry = p_end[s:s + 1, :] * carry + e_end[s:s + 1, :]
    cin = jnp.concatenate(cin, axis=0)
    return [bcum[k] + acum[k] * cin for k in range(LRU_TILES)], carry


def _lru_kernel(lu_ref, lg_ref, perm_ref, permt_ref, cw_ref, cb_ref, wg_ref, bg_ref, lam_ref, o_ref,
                lp_ref, u_ref, hs_ref, hb_ref):
    tt = lu_ref.shape[1]
    nchunk = tt // LRU_CHUNK
    chunk_rows = lambda c: slice(c * LRU_CHUNK, (c + 1) * LRU_CHUNK)
    for c in range(nchunk):
        lp_ref[chunk_rows(c), :] = _dot(perm_ref[...], lu_ref[0, chunk_rows(c), :])

    left = CONV_W // 2
    sub = lax.broadcasted_iota(jnp.int32, (SUBLANES, LRU_W), 0)

    def conv_chunk(c, _):
        base = pl.multiple_of(c * LRU_CHUNK, LRU_CHUNK)
        tile = lambda b, k: lp_ref[pl.ds(b + SUBLANES * k, SUBLANES), :]
        prev_base = pl.multiple_of(jnp.maximum(c - 1, 0) * LRU_CHUNK, LRU_CHUNK)
        next_base = pl.multiple_of(jnp.minimum(c + 1, nchunk - 1) * LRU_CHUNK, LRU_CHUNK)
        has_prev = c >= 2
        has_next = (c >= 1) & (c < nchunk - 1)

        def earlier(k):
            edge = jnp.where(has_prev, tile(prev_base, k), 0.0)
            return pltpu.roll(jnp.where(sub == SUBLANES - 1, edge, tile(base, k)), 1, 0)

        def later(k):
            edge = jnp.where(has_next, tile(next_base, k), 0.0)
            return pltpu.roll(jnp.where(sub == 0, edge, tile(base, k)), SUBLANES - 1, 0)

        for k in range(LRU_TILES):
            acc = jnp.broadcast_to(cb_ref[...], (SUBLANES, LRU_W))
            for j in range(CONV_W):
                kk = k + j - left
                if kk < 0:
                    tap = earlier(kk + LRU_TILES)
                elif kk >= LRU_TILES:
                    tap = later(kk - LRU_TILES)
                else:
                    tap = tile(base, kk)
                acc = acc + tap * cw_ref[j:j + 1, :]
            u_ref[pl.ds(base + SUBLANES * k, SUBLANES), :] = acc
        return 0

    lax.fori_loop(0, nchunk, conv_chunk, 0)

    def half_decay_scale(d):
        nlam = -lam_ref[d]
        sp = jnp.maximum(nlam, 0.0) + jnp.log(1.0 + jnp.exp(-jnp.abs(nlam)))
        return (0.5 * LRU_C) * sp

    half_c = (half_decay_scale(0), half_decay_scale(1))

    def chunk_scan(d, c, carry, h_ref):
        reverse = d == 1
        rows = pl.ds(pl.multiple_of(c * LRU_CHUNK, LRU_CHUNK), LRU_CHUNK)
        uh = u_ref[rows, :]
        th = jnp.tanh(_dot(uh.astype(BF16), wg_ref[d]) + bg_ref[d])
        neg_log_a = half_c[d] * th[:, :LRU_W] + half_c[d]
        a = jnp.exp2(neg_log_a * (-LOG2E))
        x = jnp.tanh(neg_log_a) * (a * a + 1.0)
        root = x * lax.rsqrt(jnp.maximum(x, F32_TINY))
        bterm = root * (th[:, LRU_W:] + 1.0) * uh
        split = lambda z: [z[SUBLANES * k:SUBLANES * (k + 1), :] for k in range(LRU_TILES)]
        h, carry = _scan_tiles(split(a), split(bterm), carry, reverse)
        h_ref[rows, :] = jnp.concatenate(h, axis=0)
        return carry

    def both_directions(i, carries):
        fwd = chunk_scan(0, i, carries[0], hs_ref)
        bwd = chunk_scan(1, jnp.where(i == 0, 0, nchunk - i), carries[1], hb_ref)
        return fwd, bwd

    zero = jnp.zeros((1, LRU_W), F32)
    lax.fori_loop(0, nchunk, both_directions, (zero, zero))

    k0 = float(np.sqrt(2.0 / np.pi))
    def gate_tile_order(c):
        return _dot(perm_ref[...], lg_ref[0, chunk_rows(c), :])

    z_next = gate_tile_order(0)
    for c in range(nchunk):
        z = z_next
        if c + 1 < nchunk:
            z_next = gate_tile_order(c + 1)
        gelu = z * (0.5 * (1.0 + jnp.tanh(k0 * (z + 0.044715 * (z * z * z)))))
        o_tile_order = (gelu * (hs_ref[chunk_rows(c), :] + hb_ref[chunk_rows(c), :])).astype(BF16)
        o_ref[0, chunk_rows(c), :] = _dot(permt_ref[...], o_tile_order).astype(BF16)


def _lru(layer, lu, lg, params):
    nb, tt, _ = lu.shape
    full = pl.BlockSpec((1, tt, LRU_W), lambda b: (b, 0, 0))
    perm = _chunk_permutation()
    perms = (jnp.asarray(perm, BF16), jnp.asarray(perm.T, BF16))
    consts = perms + tuple(params)
    return pl.pallas_call(
        _lru_kernel,
        grid=(nb,),
        in_specs=[full, full, *[_const_spec(a.shape) for a in perms], *[_layer_spec(a, layer) for a in params]],
        out_specs=full,
        out_shape=jax.ShapeDtypeStruct((nb, tt, LRU_W), BF16),
        scratch_shapes=[pltpu.VMEM((tt, LRU_W), F32)] * 4,
        compiler_params=pltpu.CompilerParams(vmem_limit_bytes=VMEM_LIMIT),
        name="rglru",
    )(lu, lg, *consts)


FF_CHUNK = 2048


def _mix_mlp_kernel(*refs, first_tile, nsplit):
    tokens = [refs[nsplit * k:nsplit * (k + 1)] for k in range(4)]
    (modb_ref, modc_ref, gmix_ref, gmlp_ref, wgt_ref, wna_ref, wmla_ref, wlru_ref, wo_ref, w1_ref, w2_ref,
     o_ref) = refs[4 * nsplit:]
    rows = lambda parts: parts[0][0] if nsplit == 1 else jnp.concatenate([p[0] for p in parts], axis=0)
    x, ona, omla, olru = (rows(parts) for parts in tokens)
    tm = x.shape[0]
    is_ctx = (lax.broadcasted_iota(jnp.int32, (tm, 1), 0) < CTX_LEN) & (pl.program_id(1) + first_tile == 0)
    mod = _mod_rows(modb_ref, modc_ref, is_ctx)
    h = _modulated_norm(x, gmix_ref[...], mod(0), mod(1)).astype(BF16)
    gates = _sigmoid(_dot(h, wgt_ref[...]))
    y = None
    for g, (o_g, w_r) in enumerate(((ona, wna_ref), (omla, wmla_ref), (olru, wlru_ref))):
        term = gates[:, D_MODEL * g:D_MODEL * (g + 1)] * _dot(o_g, w_r[...])
        y = term if y is None else y + term
    x1 = x + mod(2) * _dot(y.astype(BF16), wo_ref[...])

    h2 = _modulated_norm(x1, gmlp_ref[...], mod(3), mod(4)).astype(BF16)
    acc = jnp.zeros_like(x1)
    for c in range(D_FF // FF_CHUNK):
        a = jnp.maximum(_dot(h2, w1_ref[:, FF_CHUNK * c:FF_CHUNK * (c + 1)]), 0.0)
        acc = acc + _dot((a * a).astype(BF16), w2_ref[FF_CHUNK * c:FF_CHUNK * (c + 1), :])
    o_ref[0] = x1 + mod(5) * acc


def _mix_mlp(layer, xs, ona, omla, olru, mod_all, params, with_ctx):
    nb, tt, _ = xs.shape
    if with_ctx:
        tm, nsplit, first_blk = TM_MIX, 1, 0
    else:
        tm, nsplit, first_blk = 2 * CTX_LEN, 2, 1
    sub = tm // nsplit
    nt = (tt - first_blk * sub) // tm

    def tok(width):
        return [pl.BlockSpec((1, sub, width), lambda b, i, j=j: (b, nsplit * i + first_blk + j, 0))
                for j in range(nsplit)]

    token_arrays = [a for a in (xs, ona, omla, olru) for _ in range(nsplit)]
    return pl.pallas_call(
        functools.partial(_mix_mlp_kernel, first_tile=first_blk, nsplit=nsplit),
        grid=(nb, nt),
        in_specs=[
            *tok(D_MODEL), *tok(NA_WIDTH), *tok(MLA_WIDTH), *tok(LRU_W), *_mod_specs(mod_all, layer, nb),
            *[_layer_spec(a, layer) for a in params],
        ],
        out_specs=pl.BlockSpec((1, tm, D_MODEL), lambda b, i: (b, i, 0)),
        out_shape=jax.ShapeDtypeStruct((nb, nt * tm, D_MODEL), F32),
        compiler_params=pltpu.CompilerParams(vmem_limit_bytes=VMEM_LIMIT),
        name="mix_mlp",
    )(*token_arrays, mod_all, mod_all, *params)


def _group_source():
    nf = MLA_ROPE // 4
    half = LANES // 2
    src = np.full(LANES, -1)
    for part in (0, 1):
        lane0 = half * part
        src[lane0:lane0 + MLA_NOPE // 2] = MLA_NOPE // 2 * part + np.arange(MLA_NOPE // 2)
        src[lane0 + MLA_NOPE // 2:lane0 + MLA_NOPE // 2 + nf] = MLA_NOPE + nf * part + np.arange(nf)
        src[lane0 + MLA_NOPE // 2 + nf:lane0 + MLA_NOPE // 2 + 2 * nf] = MLA_NOPE + 2 * nf + nf * part + np.arange(nf)
    return src


_GROUP_SRC = _group_source()


def _to_group(v):
    return jnp.where(_GROUP_SRC >= 0, jnp.take(v, np.maximum(_GROUP_SRC, 0), axis=-1), 0)


def _pad_last(v, before, after):
    return jnp.pad(v, [(0, 0)] * (v.ndim - 1) + [(before, after)])


def _heads_to_groups(w, width):
    return _to_group(_pad_last(w, 0, MLA_QK - width)).reshape(*w.shape[:-2], MLA_HEADS * MLA_GROUP)


def _relayout_w_in(w):
    o_mr = 3 * NA_WIDTH + MLA_Q_RANK + MLA_KV_RANK
    o_gt = o_mr + MLA_ROPE + 2 * LRU_W
    mr_group = _to_group(_pad_last(w[..., o_mr:o_mr + MLA_ROPE], MLA_NOPE, 0))
    w_proj = jnp.concatenate([w[..., :o_mr], mr_group, w[..., o_mr + MLA_ROPE:o_gt]], axis=-1)
    return w_proj.astype(BF16), w[..., o_gt:].astype(BF16)


def _rope_tables(tt):
    p = np.arange(tt - CTX_LEN)
    nf = MLA_ROPE // 4
    inv = ROPE_BASE ** (-jnp.arange(nf, dtype=F32) / nf)
    cos, sin = [], []
    for pos in (jnp.asarray(p // GRID_W, F32), jnp.asarray(p % GRID_W, F32)):
        ang = pos[:, None] * inv
        cos += [jnp.cos(ang), jnp.cos(ang)]
        sin += [-jnp.sin(ang), jnp.sin(ang)]
    ones = jnp.ones((tt - CTX_LEN, MLA_NOPE), F32)
    rc = jnp.concatenate([jnp.ones((CTX_LEN, MLA_QK), F32), jnp.concatenate([ones] + cos, axis=1)], axis=0)
    rs = jnp.concatenate([jnp.zeros((CTX_LEN, MLA_QK), F32), jnp.concatenate([0.0 * ones] + sin, axis=1)], axis=0)
    return _to_group(rc), _to_group(rs)


def _block_diag(w):
    eye = jnp.eye(LRU_BLOCKS, dtype=w.dtype)
    return jnp.einsum('...ncd,nm->...ncmd', w, eye).reshape(*w.shape[:-3], LRU_W, LRU_W)


def kernel(x, c, ctx, c_ctx, w_mod, b_mod, g_mix, g_mlp, w_in, na_q_gain, na_k_gain, na_rpb, mla_qa_gain, w_q_b,
           mla_kva_gain, w_kv_b, mla_q_gain, mla_k_gain, lru_conv_w, lru_conv_b, lru_wa, lru_ba, lru_wx, lru_bx,
           lru_lambda, w_na_o, w_mla_o, w_lru_o, w_o, w_ff1, w_ff2):
    nb, seq, _ = x.shape
    depth = w_in.shape[0]
    assert ctx.shape[1] == CTX_LEN and seq // GRID_W == GRID_W // 2 and seq % GRID_W == 0
    tt = CTX_LEN + seq
    assert tt % TM_PROJ == 0 and tt % TM_MIX == 0 and tt % LRU_CHUNK == 0
    assert seq % NA_TQ == 0 and seq % MLA_TQ == 0 and seq % (2 * CTX_LEN) == 0

    xs = jnp.concatenate([ctx, x], axis=1)
    rp = -(-(nb + 1) // SUBLANES) * SUBLANES
    cvec = jnp.concatenate([c, c_ctx[None, :], jnp.zeros((rp - nb - 1, D_MODEL), F32)], axis=0)
    mod_all = _modulation(cvec, w_mod, b_mod).reshape(depth, rp, 6, D_MODEL)
    rc, rs = _rope_tables(tt)
    head_ind = jnp.asarray(np.kron(np.eye(NA_HEADS), np.ones((NA_DH, NA_DH))), BF16)

    rows = lambda v: v[:, None, :]
    bf = lambda v: v.astype(BF16)
    kvb = w_kv_b.reshape(depth, MLA_KV_RANK, MLA_HEADS, MLA_NOPE + MLA_V)
    w_proj, w_gate = _relayout_w_in(bf(w_in))
    proj_params = (
        rows(g_mix), w_proj,
        rows(jnp.tile(na_q_gain, (1, NA_HEADS)) * (NA_DH ** -0.5 * LOG2E)), rows(jnp.tile(na_k_gain, (1, NA_HEADS))),
        rows(mla_qa_gain), bf(_heads_to_groups(w_q_b.reshape(depth, MLA_Q_RANK, MLA_HEADS, MLA_QK), MLA_QK)),
        rows(mla_kva_gain),
        bf(jnp.concatenate([_heads_to_groups(kvb[..., :MLA_NOPE], MLA_NOPE),
                            kvb[..., MLA_NOPE:].reshape(depth, MLA_KV_RANK, MLA_WIDTH)], axis=-1)),
        rows(_to_group(mla_q_gain) * (MLA_QK ** -0.5 * LOG2E)), rows(_to_group(mla_k_gain)))
    na_bias = _na_bias_table(na_rpb)
    lru_params = (
        0.5 * lru_conv_w, rows(0.5 * lru_conv_b),
        bf(jnp.concatenate([_block_diag(lru_wa), _block_diag(lru_wx)], axis=-1)),
        0.5 * jnp.concatenate([lru_ba, lru_bx], axis=-1)[:, :, None, :], lru_lambda[:, :, None, :])
    mix_params = (rows(g_mix), rows(g_mlp), w_gate, bf(w_na_o), bf(w_mla_o), bf(w_lru_o), bf(w_o), bf(w_ff1),
                  bf(w_ff2))

    for i in range(depth):
        with_ctx = i < depth - 1
        naq, nak, nav, qm, km, vm, lu, lg = _inproj(i, xs, mod_all, head_ind, rc, rs, proj_params)
        o_na = _na_attention(i, naq, nak, nav, na_bias, with_ctx)
        o_mla = _mla_attention(qm, km, vm, with_ctx)
        o_lru = _lru(i, lu, lg, lru_params)
        xs = _mix_mlp(i, xs, o_na, o_mla, o_lru, mod_all, mix_params, with_ctx)
    return xs
```
